```python
import functools
import jax, jax.numpy as jnp
from jax import lax
import numpy as np

D_MODEL = 1024
BATCH = 8
SEQ = 2048
DEPTH = 4
DEC_BATCH = 128
DEC_SEQ = 4
PAST_LEN = 2048
PAGE_SIZE = 128

HEAD_DIM = 64
N_HEADS = D_MODEL // HEAD_DIM
N_KV = 4
GROUP = N_HEADS // N_KV
Q_DIM = N_HEADS * HEAD_DIM
KV_DIM = N_KV * HEAD_DIM
CMP_BLOCK = 32
CMP_STRIDE = 16
CMP_HID = 2 * HEAD_DIM
SEL_BLOCK = 64
TOP_N = 16
WINDOW = 512
CONV_DIM = D_MODEL
CONV_W = 3
FF_DIM = 4 * D_MODEL
IN_DIM = Q_DIM + 6 * KV_DIM + 3 * N_HEADS + 3 * CONV_DIM + 2 * D_MODEL
Q_BLOCK = 32
EPS = 1e-6
NEG = -1e30
FORCE_BONUS = 1e4

kernel_name = "hybrid_nsa_shortconv_decode_step"


def rmsnorm(x, g):
    x32 = x.astype(jnp.float32)
    y = x32 * lax.rsqrt(jnp.mean(x32 * x32, axis=-1, keepdims=True) + EPS)
    return (y * g.astype(jnp.float32)).astype(x.dtype)


def masked_softmax(s, mask):
    s = jnp.where(mask, s.astype(jnp.float32), NEG)
    e = jnp.where(mask, jnp.exp(s - s.max(axis=-1, keepdims=True)), 0.0)
    return e / jnp.maximum(e.sum(axis=-1, keepdims=True), 1e-30)


def project(h, w_in):
    B, T, _ = h.shape
    sizes = [Q_DIM, 2 * KV_DIM, 2 * KV_DIM, 2 * KV_DIM, 3 * N_HEADS, CONV_DIM, CONV_DIM, CONV_DIM, 2 * D_MODEL]
    offsets = [sum(sizes[:i]) for i in range(1, len(sizes))]
    q, kv_c, kv_s, kv_w, hg, b_gate, c_gate, x_in, merge_logits = jnp.split(h @ w_in, offsets, axis=-1)
    kv = lambda t: t.reshape(B, T, 2, N_KV, HEAD_DIM)
    return (q.reshape(B, T, N_KV, GROUP, HEAD_DIM), kv(kv_c), kv(kv_s), kv(kv_w),
            jax.nn.sigmoid(hg.reshape(B, T, N_KV, GROUP, 3)), b_gate, c_gate, x_in, merge_logits)


def compress(rows, pe, w1, w2):
    B, L, G, dh = rows.shape
    n_chunks = L // CMP_STRIDE
    r = CMP_BLOCK // CMP_STRIDE
    n_cmp = n_chunks - r + 1
    ch = rows[:, :n_chunks * CMP_STRIDE].reshape(B, n_chunks, CMP_STRIDE, G, dh)
    blocks = jnp.concatenate([ch[:, i:i + n_cmp] for i in range(r)], axis=2)
    blocks = blocks + pe[:, None, :]
    flat = blocks.transpose(0, 1, 3, 2, 4).reshape(B, n_cmp, G, CMP_BLOCK * dh)
    return jax.nn.relu(flat @ w1) @ w2


def sel_blocks(rows):
    B, L, G, dh = rows.shape
    n_sel = -(-L // SEL_BLOCK)
    rows = jnp.pad(rows, ((0, 0), (0, n_sel * SEL_BLOCK - L), (0, 0), (0, 0)))
    return rows.reshape(B, n_sel, SEL_BLOCK, G, dh).transpose(0, 3, 1, 2, 4)


def sel_map(n_cmp, n_sel):
    cs = np.arange(n_cmp)[:, None] * CMP_STRIDE
    ss = np.arange(n_sel)[None, :] * SEL_BLOCK
    ov = np.clip(np.minimum(cs + CMP_BLOCK, ss + SEL_BLOCK) - np.maximum(cs, ss), 0, None) / CMP_STRIDE
    return jnp.asarray(ov, dtype=jnp.float32)


def nsa_core(q, qpos, gates, kc, vc, ks_b, vs_b, kw, vw, wpos, smap):
    B, Tq, G, R, dh = q.shape
    scale = dh ** -0.5
    n_cmp = kc.shape[1]
    cmp_end = jnp.arange(n_cmp, dtype=jnp.int32) * CMP_STRIDE + CMP_BLOCK - 1
    cmask = (cmp_end[None, :] <= qpos[:, None])[None, :, None, None, :]
    p_cmp = masked_softmax(jnp.einsum("btgrd,bngd->btgrn", q, kc) * scale, cmask)
    o_cmp = jnp.einsum("btgrn,bngd->btgrd", p_cmp.astype(vc.dtype), vc)
    n_sel = smap.shape[1]
    imp = p_cmp.sum(axis=3) @ smap
    blk = jnp.arange(n_sel, dtype=jnp.int32)[None, :]
    cur = (qpos // SEL_BLOCK)[:, None]
    forced = (blk == 0) | (blk == cur) | (blk == cur - 1)
    causal_blk = blk * SEL_BLOCK <= qpos[:, None]
    imp = jnp.where(causal_blk[None, :, None, :], imp + jnp.where(forced, FORCE_BONUS, 0.0)[None, :, None, :], NEG)
    _, idx = lax.top_k(imp, min(TOP_N, n_sel))
    n_top = idx.shape[-1]
    b_i = jnp.arange(B)[:, None, None, None]
    g_i = jnp.arange(G)[None, None, :, None]
    ksel = ks_b[b_i, g_i, idx]
    vsel = vs_b[b_i, g_i, idx].reshape(B, Tq, G, n_top * SEL_BLOCK, dh)
    kpos = idx[..., None] * SEL_BLOCK + jnp.arange(SEL_BLOCK, dtype=jnp.int32)
    smask = (kpos <= qpos[None, :, None, None, None]).reshape(B, Tq, G, 1, n_top * SEL_BLOCK)
    s_sel = jnp.einsum("btgrd,btgnkd->btgrnk", q, ksel).reshape(B, Tq, G, R, n_top * SEL_BLOCK) * scale
    p_sel = masked_softmax(s_sel, smask)
    o_sel = jnp.einsum("btgrm,btgmd->btgrd", p_sel.astype(vsel.dtype), vsel)
    wmask = (wpos[None, :] <= qpos[:, None]) & (wpos[None, :] > qpos[:, None] - WINDOW) & (wpos[None, :] >= 0)
    p_win = masked_softmax(jnp.einsum("btgrd,bsgd->btgrs", q, kw) * scale, wmask[None, :, None, None, :])
    o_win = jnp.einsum("btgrs,bsgd->btgrd", p_win.astype(vw.dtype), vw)
    return gates[..., 0:1] * o_cmp + gates[..., 1:2] * o_sel + gates[..., 2:3] * o_win


def prompt_attention(q, gates, kv_c, kv_s, kv_w, pe, w1, w2):
    B, T = q.shape[:2]
    kc = compress(kv_c[:, :, 0], pe[0], w1[0], w2[0])
    vc = compress(kv_c[:, :, 1], pe[1], w1[1], w2[1])
    ks_b = sel_blocks(kv_s[:, :, 0])
    vs_b = sel_blocks(kv_s[:, :, 1])
    smap = sel_map(kc.shape[1], ks_b.shape[2])
    kw_pad = jnp.pad(kv_w, ((0, 0), (WINDOW, 0), (0, 0), (0, 0), (0, 0)))
    n_qb = T // Q_BLOCK
    qb = q.reshape(B, n_qb, Q_BLOCK, N_KV, GROUP, HEAD_DIM).swapaxes(0, 1)
    gb = gates.reshape(B, n_qb, Q_BLOCK, N_KV, GROUP, 3).swapaxes(0, 1)

    def block(args):
        q_i, g_i, i = args
        start = i * Q_BLOCK
        qpos = start + jnp.arange(Q_BLOCK, dtype=jnp.int32)
        kw = lax.dynamic_slice_in_dim(kw_pad, start, WINDOW + Q_BLOCK, axis=1)
        wpos = start - WINDOW + jnp.arange(WINDOW + Q_BLOCK, dtype=jnp.int32)
        return nsa_core(q_i, qpos, g_i, kc, vc, ks_b, vs_b, kw[:, :, 0], kw[:, :, 1], wpos, smap)

    o = lax.map(block, (qb, gb, jnp.arange(n_qb, dtype=jnp.int32)))
    return o.swapaxes(0, 1).reshape(B, T, N_KV, GROUP, HEAD_DIM), kv_w[:, -min(WINDOW, T):]


def sample_attention(q, gates, kv_c, kv_s, kv_w, cache_c, cache_s, win_buf, page_table, pe, w1, w2):
    B, T = q.shape[:2]
    n_pages = page_table.shape[1]
    past_len = n_pages * PAGE_SIZE

    def gather(cache):
        return cache[page_table].reshape(B, past_len, 2, N_KV, HEAD_DIM)

    full_c = jnp.concatenate([gather(cache_c), kv_c], axis=1)
    full_s = jnp.concatenate([gather(cache_s), kv_s], axis=1)
    kc = compress(full_c[:, :, 0], pe[0], w1[0], w2[0])
    vc = compress(full_c[:, :, 1], pe[1], w1[1], w2[1])
    ks_b = sel_blocks(full_s[:, :, 0])
    vs_b = sel_blocks(full_s[:, :, 1])
    smap = sel_map(kc.shape[1], ks_b.shape[2])
    wb = win_buf.shape[1]
    kw_all = jnp.concatenate([win_buf, kv_w], axis=1)
    wpos = past_len - wb + jnp.arange(wb + T, dtype=jnp.int32)
    qpos = past_len + jnp.arange(T, dtype=jnp.int32)
    o = nsa_core(q, qpos, gates, kc, vc, ks_b, vs_b, kw_all[:, :, 0], kw_all[:, :, 1], wpos, smap)
    return o, kw_all[:, -wb:]


def short_conv(u_ext, conv_w, T):
    return sum(u_ext[:, k:k + T] * conv_w[k] for k in range(CONV_W))


def squared_relu_mlp(h, w1, w2):
    return jnp.square(jax.nn.relu(h @ w1)) @ w2


def layer(x, attend, conv_prefix, g, w_in, conv_w, w_up_attn, w_up_conv, w_o, w_ff1, w_ff2):
    B, T, _ = x.shape
    h = rmsnorm(x, g[0])
    q, kv_c, kv_s, kv_w, head_gates, b_gate, c_gate, x_in, merge_logits = project(h, w_in)
    o_attn, win_state = attend(q, head_gates, kv_c, kv_s, kv_w)
    u_ext = jnp.concatenate([conv_prefix, c_gate * x_in], axis=1)
    y_conv = b_gate * short_conv(u_ext, conv_w, T)
    gate_a, gate_c = jnp.split(jax.nn.sigmoid(merge_logits), 2, axis=-1)
    mix = (gate_a * (o_attn.reshape(B, T, Q_DIM) @ w_up_attn) + gate_c * (y_conv @ w_up_conv)) @ w_o
    x = x + rmsnorm(mix, g[1])
    x = x + rmsnorm(squared_relu_mlp(rmsnorm(x, g[2]), w_ff1, w_ff2), g[3])
    return x, kv_c, kv_s, win_state, u_ext[:, -(CONV_W - 1):]


def setup_inputs(seed: int = 0) -> dict:
    key = jax.random.key(seed)
    ks = jax.random.split(key, 20)
    n_pages = PAST_LEN // PAGE_SIZE
    n_used = DEC_BATCH * n_pages
    n_pool = (n_used * 5 + 3) // 4
    win_buf = min(WINDOW, PAST_LEN)
    nrm = lambda k, shape, scale: jax.random.normal(k, shape, jnp.float32) * scale
    page_table = jax.random.permutation(ks[0], n_pool)[:n_used].reshape(DEC_BATCH, n_pages).astype(jnp.int32)
    return {
        "x_prompt": nrm(ks[1], (BATCH, SEQ, D_MODEL), 1.0),
        "x_sample": nrm(ks[2], (DEC_BATCH, DEC_SEQ, D_MODEL), 1.0),
        "cache_cmp_kv": nrm(ks[3], (DEPTH, n_pool, PAGE_SIZE, 2, N_KV, HEAD_DIM), 1.0),
        "cache_sel_kv": nrm(ks[4], (DEPTH, n_pool, PAGE_SIZE, 2, N_KV, HEAD_DIM), 1.0),
        "state_win_kv": nrm(ks[5], (DEPTH, DEC_BATCH, win_buf, 2, N_KV, HEAD_DIM), 1.0),
        "state_conv": nrm(ks[6], (DEPTH, DEC_BATCH, CONV_W - 1, CONV_DIM), 1.0),
        "page_table": page_table,
        "norm_g": 1.0 + nrm(ks[7], (DEPTH, 4, D_MODEL), 0.1),
        "w_in": nrm(ks[8], (DEPTH, D_MODEL, IN_DIM), D_MODEL ** -0.5),
        "cmp_pe": nrm(ks[9], (DEPTH, 2, CMP_BLOCK, HEAD_DIM), 0.1),
        "cmp_w1": nrm(ks[10], (DEPTH, 2, CMP_BLOCK * HEAD_DIM, CMP_HID), (CMP_BLOCK * HEAD_DIM) ** -0.5),
        "cmp_w2": nrm(ks[11], (DEPTH, 2, CMP_HID, HEAD_DIM), CMP_HID ** -0.5),
        "conv_w": nrm(ks[12], (DEPTH, CONV_W, CONV_DIM), CONV_W ** -0.5),
        "w_up_attn": nrm(ks[13], (DEPTH, Q_DIM, D_MODEL), Q_DIM ** -0.5),
        "w_up_conv": nrm(ks[14], (DEPTH, CONV_DIM, D_MODEL), CONV_DIM ** -0.5),
        "w_o": nrm(ks[15], (DEPTH, D_MODEL, D_MODEL), D_MODEL ** -0.5),
        "w_ff1": nrm(ks[16], (DEPTH, D_MODEL, FF_DIM), D_MODEL ** -0.5),
        "w_ff2": nrm(ks[17], (DEPTH, FF_DIM, D_MODEL), FF_DIM ** -0.5),
    }


def reference(x_prompt, x_sample, cache_cmp_kv, cache_sel_kv, state_win_kv, state_conv, page_table,
              norm_g, w_in, cmp_pe, cmp_w1, cmp_w2, conv_w, w_up_attn, w_up_conv, w_o, w_ff1, w_ff2):
    xp, xs = x_prompt, x_sample
    cmp_p, cmp_s, sel_p, sel_s, win_p, win_s, conv_p, conv_s = [], [], [], [], [], [], [], []
    for l in range(DEPTH):
        shared = (norm_g[l], w_in[l], conv_w[l], w_up_attn[l], w_up_conv[l], w_o[l], w_ff1[l], w_ff2[l])
        attend_p = functools.partial(prompt_attention, pe=cmp_pe[l], w1=cmp_w1[l], w2=cmp_w2[l])
        prefix_p = jnp.zeros((xp.shape[0], CONV_W - 1, CONV_DIM), xp.dtype)
        xp, c, s, w, cv = layer(xp, attend_p, prefix_p, *shared)
        cmp_p.append(c); sel_p.append(s); win_p.append(w); conv_p.append(cv)
        attend_s = functools.partial(sample_attention, cache_c=cache_cmp_kv[l], cache_s=cache_sel_kv[l],
                                     win_buf=state_win_kv[l], page_table=page_table,
                                     pe=cmp_pe[l], w1=cmp_w1[l], w2=cmp_w2[l])
        xs, c, s, w, cv = layer(xs, attend_s, state_conv[l], *shared)
        cmp_s.append(c); sel_s.append(s); win_s.append(w); conv_s.append(cv)
    return (xp, xs, jnp.stack(cmp_p), jnp.stack(cmp_s), jnp.stack(sel_p), jnp.stack(sel_s),
            jnp.stack(win_p), jnp.stack(win_s), jnp.stack(conv_p), jnp.stack(conv_s))
```

```python
import functools

import numpy as np
import jax
import jax.numpy as jnp
from jax import lax
from jax.experimental import pallas as pl
from jax.experimental.pallas import tpu as pltpu

F32 = jnp.float32
BF16 = jnp.bfloat16

HEAD_DIM = 64
N_KV = 4
GROUP = 4
N_HEADS = N_KV * GROUP
KV_DIM = N_KV * HEAD_DIM
KV_ROW = 2 * KV_DIM
PAIR = 2 * HEAD_DIM
CMP_BLOCK = 32
CMP_STRIDE = 16
CMP_HID = 2 * HEAD_DIM
SEL_BLOCK = 64
TOP_N = 16
WINDOW = 512
CONV_W = 3
EPS = 1e-6
NEG = -1e30
FORCE_BONUS = 1e4
LANES = 128
VMEM_LIMIT = 56 * 1024 * 1024


def _rms(x, g):
    return x * lax.rsqrt(jnp.mean(x * x, axis=-1, keepdims=True) + EPS) * g


def _sigmoid(x):
    return 1.0 / (1.0 + jnp.exp(-x))


def _dot(a, b):
    return jnp.dot(a, b, preferred_element_type=F32)


def _dot_nt(a, b):
    return lax.dot_general(a, b, (((1,), (1,)), ((), ())), preferred_element_type=F32)


def _const_spec(shape):
    nd = len(shape)
    return pl.BlockSpec(shape, lambda *_: (0,) * nd, pipeline_mode=pl.Buffered(1))


_D = 1024
_OFF_Q, _OFF_B, _OFF_C, _OFF_X, _OFF_GA, _OFF_GC = 0, _D, 2 * _D, 3 * _D, 4 * _D, 5 * _D
_OFF_KVC = 6 * _D
_OFF_KVS = _OFF_KVC + KV_ROW
_OFF_KVW = _OFF_KVS + KV_ROW
_OFF_HG = _OFF_KVW + KV_ROW
_NP = _OFF_HG + LANES


def _proj_kernel(*refs, carry_mode, tiles_per_seq, seq_len):
    if carry_mode:
        (x_ref, g_ref, w_ref, cw_ref,
         q_ref, yc_ref, ga_ref, gc_ref, kvc_ref, kvs_ref, kvw_ref, kvsb_ref, kvwb_ref, hg_ref, ust_ref,
         carry_ref) = refs
    else:
        (x_ref, g_ref, w_ref, cw_ref, p1_ref, p2_ref,
         q_ref, yc_ref, ga_ref, gc_ref, kvc_ref, kvs_ref, kvw_ref, kvsb_ref, kvwb_ref, hg_ref, ust_ref) = refs
    tm = x_ref.shape[0]
    h = _rms(x_ref[...], g_ref[...]).astype(BF16)

    def mm(off, width):
        return _dot(h, w_ref[:, off:off + width])

    q_ref[...] = (mm(_OFF_Q, _D) * (HEAD_DIM ** -0.5)).astype(BF16)
    u = mm(_OFF_C, _D) * mm(_OFF_X, _D)
    row = lax.broadcasted_iota(jnp.int32, (tm, 1), 0)
    r1 = pltpu.roll(u, 1, axis=0)
    r2 = pltpu.roll(u, 2, axis=0)
    if carry_mode:
        @pl.when(pl.program_id(0) % tiles_per_seq == 0)
        def _():
            carry_ref[...] = jnp.zeros_like(carry_ref)
        c6 = carry_ref[6:7, :]
        c7 = carry_ref[7:8, :]
        u_m1 = jnp.where(row == 0, c7, r1)
        u_m2 = jnp.where(row == 0, c6, jnp.where(row == 1, c7, r2))
        tail = u[tm - 8:tm, :]
        carry_ref[...] = tail
        ust_ref[0] = tail
    else:
        t_in = row & (seq_len - 1)
        u_m1 = jnp.where(t_in == 0, p1_ref[...], r1)
        u_m2 = jnp.where(t_in < 2, p2_ref[...], r2)
        ust_ref[...] = u
    cw = cw_ref[...]
    yc_ref[...] = mm(_OFF_B, _D) * (cw[0:1] * u_m2 + cw[1:2] * u_m1 + cw[2:3] * u)
    ga_ref[...] = _sigmoid(mm(_OFF_GA, _D))
    gc_ref[...] = _sigmoid(mm(_OFF_GC, _D))
    kvc_ref[...] = mm(_OFF_KVC, KV_ROW)
    kvs = mm(_OFF_KVS, KV_ROW)
    kvs_ref[...] = kvs
    kvsb_ref[...] = kvs.astype(BF16)
    kvw = mm(_OFF_KVW, KV_ROW)
    kvw_ref[...] = kvw
    kvwb_ref[...] = kvw.astype(BF16)
    hg_ref[...] = _sigmoid(mm(_OFF_HG, LANES))


def _proj(x, g, w, cw, *, tm, seq_len, prefix=None):
    M, D = x.shape
    carry_mode = prefix is None
    n_tiles = M // tm
    row_spec = lambda width: pl.BlockSpec((tm, width), lambda i: (i, 0))
    in_specs = [row_spec(D), _const_spec((1, D)), _const_spec((D, _NP)), _const_spec((CONV_W, D))]
    args = [x, g, w, cw]
    scratch = []
    if carry_mode:
        assert seq_len % tm == 0
        tiles_per_seq = seq_len // tm
        ust_shape = jax.ShapeDtypeStruct((M // seq_len, 8, D), F32)
        ust_spec = pl.BlockSpec((1, 8, D), lambda i: (i // tiles_per_seq, 0, 0))
        scratch = [pltpu.VMEM((8, D), F32)]
    else:
        assert tm % seq_len == 0 and seq_len >= CONV_W - 1
        tiles_per_seq = 1
        in_specs += [row_spec(D), row_spec(D)]
        args += list(prefix)
        ust_shape = jax.ShapeDtypeStruct((M, D), F32)
        ust_spec = row_spec(D)
    out_shape = [
        jax.ShapeDtypeStruct((M, D), BF16),
        jax.ShapeDtypeStruct((M, D), F32),
        jax.ShapeDtypeStruct((M, D), F32),
        jax.ShapeDtypeStruct((M, D), F32),
        jax.ShapeDtypeStruct((M, KV_ROW), F32),
        jax.ShapeDtypeStruct((M, KV_ROW), F32),
        jax.ShapeDtypeStruct((M, KV_ROW), F32),
        jax.ShapeDtypeStruct((M, KV_ROW), BF16),
        jax.ShapeDtypeStruct((M, KV_ROW), BF16),
        jax.ShapeDtypeStruct((M, LANES), F32),
        ust_shape,
    ]
    out_specs = [row_spec(D)] * 4 + [row_spec(KV_ROW)] * 5 + [row_spec(LANES), ust_spec]
    return pl.pallas_call(
        functools.partial(_proj_kernel, carry_mode=carry_mode, tiles_per_seq=tiles_per_seq, seq_len=seq_len),
        grid=(n_tiles,), in_specs=in_specs, out_specs=out_specs, out_shape=out_shape,
        scratch_shapes=scratch, name="proj_carry" if carry_mode else "proj_prefix",
        compiler_params=pltpu.CompilerParams(dimension_semantics=("arbitrary",), vmem_limit_bytes=VMEM_LIMIT),
    )(*args)


def _pe_bias(pe_ref, w1_ref, kv):
    return _dot(pe_ref[kv], w1_ref[kv])[0:1, :]


def _compress_pair(load_rows, wp_ref, w2_ref, peb, kv, n_chunks):
    acc = jnp.zeros((n_chunks, 4 * CMP_HID), F32)
    for pos in range(CMP_STRIDE):
        acc = acc + _dot(load_rows(pos).astype(BF16), wp_ref[kv, pos])
    hid = []
    for gl in range(2):
        a = acc[:, gl * 2 * CMP_HID: gl * 2 * CMP_HID + CMP_HID]
        b = acc[:, gl * 2 * CMP_HID + CMP_HID: (gl + 1) * 2 * CMP_HID]
        hid.append(jnp.maximum(a + pltpu.roll(b, n_chunks - 1, axis=0) + peb, 0.0))
    return _dot(jnp.concatenate(hid, axis=1).astype(BF16), w2_ref[kv])


def _compress_kernel(src_ref, wp_ref, w1_ref, w2_ref, pe_ref, out_ref, *, n_chunks):
    j = pl.program_id(1)
    load = lambda pos: src_ref[0, pl.ds(pos, n_chunks, stride=CMP_STRIDE), :]
    for kv in range(2):
        @pl.when(j // 2 == kv)
        def _():
            peb = _pe_bias(pe_ref, w1_ref, kv)
            out_ref[0] = _compress_pair(load, wp_ref, w2_ref, peb, kv, n_chunks).astype(BF16)


def _compress(kvc, wp, w1, w2, pe):
    B, T, _ = kvc.shape
    n_chunks = T // CMP_STRIDE
    return pl.pallas_call(
        functools.partial(_compress_kernel, n_chunks=n_chunks),
        grid=(B, KV_ROW // PAIR),
        in_specs=[pl.BlockSpec((1, T, PAIR), lambda b, j: (b, 0, j)),
                  _const_spec(wp.shape), _const_spec(w1.shape), _const_spec(w2.shape), _const_spec(pe.shape)],
        out_specs=pl.BlockSpec((1, n_chunks, PAIR), lambda b, j: (b, 0, j)),
        out_shape=jax.ShapeDtypeStruct((B, n_chunks, KV_ROW), BF16),
        name="compress",
        compiler_params=pltpu.CompilerParams(dimension_semantics=("arbitrary", "arbitrary"),
                                             vmem_limit_bytes=VMEM_LIMIT),
    )(kvc, wp, w1, w2, pe)


def _stack_pair(pieces_g0, pieces_g1):
    return jnp.concatenate(list(pieces_g0) + list(pieces_g1), axis=0)


def _flash(q2, chunks, get_k, get_v, mask_fn, m_ref, l_ref, acc_ref, kc):
    m_ref[...] = jnp.full(m_ref.shape, NEG, F32)
    l_ref[...] = jnp.zeros(l_ref.shape, F32)
    acc_ref[...] = jnp.zeros(acc_ref.shape, F32)

    def step(c):
        s = _dot_nt(q2, get_k(c))
        msk = mask_fn(c)
        s = jnp.where(msk, s, NEG)
        m_old = m_ref[...]
        m_new = jnp.maximum(m_old, jnp.max(s, axis=-1, keepdims=True))
        alpha = jnp.exp(m_old - m_new)
        p = jnp.where(msk, jnp.exp(s - m_new), 0.0)
        l_ref[...] = alpha * l_ref[...] + jnp.sum(p, axis=-1, keepdims=True)
        acc_ref[...] = alpha * acc_ref[...] + _dot(p.astype(BF16), get_v(c))
        m_ref[...] = m_new

    if isinstance(chunks, tuple):
        lo, hi = chunks

        def body(c, carry):
            step(c)
            return carry
        lax.fori_loop(lo, hi, body, 0)
    else:
        for c in chunks:
            step(c)
    return acc_ref[...] / jnp.maximum(l_ref[...], 1e-30)


def _nsa_pair(*, q2, qpos, tq, kc2, vc2, n_cmp, n_sel, smap_ref, hg, head0,
              sel_chunks, sel_k, sel_v, win_chunks, win_k, win_v, win_base, kc,
              m_ref, l_ref, acc_ref):
    M = 8 * tq
    s = _dot_nt(q2, kc2)
    n_idx = lax.broadcasted_iota(jnp.int32, (1, kc2.shape[0]), 1)
    cmask = (n_idx * CMP_STRIDE + (CMP_BLOCK - 1) <= qpos) & (n_idx < n_cmp)
    s = jnp.where(cmask, s, NEG)
    e = jnp.where(cmask, jnp.exp(s - jnp.max(s, axis=-1, keepdims=True)), 0.0)
    p_cmp = e / jnp.maximum(jnp.sum(e, axis=-1, keepdims=True), 1e-30)
    o_cmp = _dot(p_cmp.astype(BF16), vc2)

    lane = lax.broadcasted_iota(jnp.int32, (1, LANES), 1)
    qpos_t = qpos[0:tq, :]
    cur = qpos_t >> 6
    forced = (lane == 0) | (lane == cur) | (lane == cur - 1)
    causal_blk = lane * SEL_BLOCK <= qpos_t
    n_top = min(TOP_N, n_sel)
    sel_rows = []
    for gl in range(2):
        psum = p_cmp[(gl * GROUP) * tq:(gl * GROUP + 1) * tq, :]
        for r in range(1, GROUP):
            psum = psum + p_cmp[(gl * GROUP + r) * tq:(gl * GROUP + r + 1) * tq, :]
        hi = psum.astype(BF16)
        lo = (psum - hi.astype(F32)).astype(BF16)
        imp = _dot(hi, smap_ref[...]) + _dot(lo, smap_ref[...])
        imp = jnp.where(causal_blk, imp + jnp.where(forced, FORCE_BONUS, 0.0), NEG)
        imp = jnp.where(lane < n_sel, imp, -jnp.inf)
        cnt = jnp.zeros((tq, LANES), F32)
        for i in range(n_sel):
            ci = imp[:, i:i + 1]
            tie = jnp.where(lane > i, 1.0, 0.0)
            cnt = cnt + jnp.where(ci > imp, 1.0, jnp.where(ci == imp, tie, 0.0))
        sel_rows.append(jnp.where(cnt < n_top, 1.0, 0.0).astype(BF16))

    blk_row = lax.broadcasted_iota(jnp.int32, (LANES, 1), 0)
    key_lane = lax.broadcasted_iota(jnp.int32, (1, kc), 1)

    def sel_mask(c):
        kpos = c * kc + key_lane
        expand = jnp.where(blk_row == (kpos >> 6), 1.0, 0.0).astype(BF16)
        flags = _stack_pair([_dot(sel_rows[0], expand)] * GROUP, [_dot(sel_rows[1], expand)] * GROUP)
        return (flags > 0.5) & (kpos <= qpos)

    o_sel = _flash(q2, sel_chunks, sel_k, sel_v, sel_mask, m_ref, l_ref, acc_ref, kc)

    def win_mask(c):
        kpos = win_base + c * kc + key_lane
        return (kpos <= qpos) & (kpos > qpos - WINDOW) & (kpos >= 0)

    o_win = _flash(q2, win_chunks, win_k, win_v, win_mask, m_ref, l_ref, acc_ref, kc)

    def gate_col(br):
        cols = [hg[:, (head0 + j) * 3 + br:(head0 + j) * 3 + br + 1] for j in range(2 * GROUP)]
        return jnp.concatenate(cols, axis=0)
    return gate_col(0) * o_cmp + gate_col(1) * o_sel + gate_col(2) * o_win


def _build_q2(q, pr, tq):
    zero = jnp.zeros((tq, HEAD_DIM), F32)
    g0, g1 = [], []
    for r in range(GROUP):
        h0 = (2 * pr) * GROUP + r
        h1 = (2 * pr + 1) * GROUP + r
        g0.append(jnp.concatenate([q[:, h0 * HEAD_DIM:(h0 + 1) * HEAD_DIM].astype(F32), zero], axis=1))
        g1.append(jnp.concatenate([zero, q[:, h1 * HEAD_DIM:(h1 + 1) * HEAD_DIM].astype(F32)], axis=1))
    return _stack_pair(g0, g1).astype(BF16)


def _store_heads(o_ref_setter, gated, pr, tq):
    lane = lax.broadcasted_iota(jnp.int32, (1, LANES), 1)
    for gl in range(2):
        for half in range(GROUP // 2):
            r_e, r_o = 2 * half, 2 * half + 1
            even = gated[(gl * GROUP + r_e) * tq:(gl * GROUP + r_e + 1) * tq, :]
            odd = gated[(gl * GROUP + r_o) * tq:(gl * GROUP + r_o + 1) * tq, :]
            if gl == 0:
                odd = pltpu.roll(odd, HEAD_DIM, axis=1)
            else:
                even = pltpu.roll(even, HEAD_DIM, axis=1)
            block = jnp.where(lane < HEAD_DIM, even, odd)
            o_ref_setter((2 * pr + gl) * (GROUP // 2) + half, block)


def _attn_prompt_kernel(q_ref, hg_ref, cmp_ref, kvs_ref, kvw_ref, smap_ref, o_ref, m_ref, l_ref, acc_ref,
                        *, tq, kc, n_cmp, n_sel):
    i = pl.program_id(1)
    q0 = i * tq
    M = 8 * tq
    row = lax.broadcasted_iota(jnp.int32, (M, 1), 0)
    qpos = q0 + (row & (tq - 1))
    hi = (q0 + tq + kc - 1) // kc
    win_lo = jnp.maximum(q0 - WINDOW, 0) // kc
    q = q_ref[...]
    hg = hg_ref[...]

    def setter(blk, val):
        o_ref[:, blk * LANES:(blk + 1) * LANES] = val.astype(o_ref.dtype)

    for pr in range(2):
        k_lane, v_lane = pr * PAIR, KV_DIM + pr * PAIR
        rows = lambda c: pl.ds(pl.multiple_of(c * kc, kc), kc)
        gated = _nsa_pair(
            q2=_build_q2(q, pr, tq), qpos=qpos, tq=tq,
            kc2=cmp_ref[0, :, k_lane:k_lane + PAIR], vc2=cmp_ref[0, :, v_lane:v_lane + PAIR],
            n_cmp=n_cmp, n_sel=n_sel, smap_ref=smap_ref, hg=hg, head0=2 * pr * GROUP,
            sel_chunks=(0, hi),
            sel_k=lambda c: kvs_ref[0, rows(c), pl.ds(k_lane, PAIR)],
            sel_v=lambda c: kvs_ref[0, rows(c), pl.ds(v_lane, PAIR)],
            win_chunks=(win_lo, hi),
            win_k=lambda c: kvw_ref[0, rows(c), pl.ds(k_lane, PAIR)],
            win_v=lambda c: kvw_ref[0, rows(c), pl.ds(v_lane, PAIR)],
            win_base=0, kc=kc, m_ref=m_ref, l_ref=l_ref, acc_ref=acc_ref)
        _store_heads(setter, gated, pr, tq)


def _attn_prompt(q, hg, cmp_kv, kvs_b, kvw_b, smap, *, B, T, tq, kc):
    n_t = T // tq
    n_chunks = T // CMP_STRIDE
    n_cmp = n_chunks - CMP_BLOCK // CMP_STRIDE + 1
    n_sel = -(-T // SEL_BLOCK)
    assert T % kc == 0 and T % tq == 0 and n_sel <= LANES and n_chunks <= LANES
    M = 8 * tq
    D = q.shape[1]
    return pl.pallas_call(
        functools.partial(_attn_prompt_kernel, tq=tq, kc=kc, n_cmp=n_cmp, n_sel=n_sel),
        grid=(B, n_t),
        in_specs=[pl.BlockSpec((tq, D), lambda b, i: (b * n_t + i, 0)),
                  pl.BlockSpec((tq, LANES), lambda b, i: (b * n_t + i, 0)),
                  pl.BlockSpec((1, n_chunks, KV_ROW), lambda b, i: (b, 0, 0)),
                  pl.BlockSpec((1, T, KV_ROW), lambda b, i: (b, 0, 0)),
                  pl.BlockSpec((1, T, KV_ROW), lambda b, i: (b, 0, 0)),
                  _const_spec(smap.shape)],
        out_specs=pl.BlockSpec((tq, D), lambda b, i: (b * n_t + i, 0)),
        out_shape=jax.ShapeDtypeStruct((B * T, D), BF16),
        scratch_shapes=[pltpu.VMEM((M, 1), F32), pltpu.VMEM((M, 1), F32), pltpu.VMEM((M, PAIR), F32)],
        name="attn_prompt",
        compiler_params=pltpu.CompilerParams(dimension_semantics=("arbitrary", "arbitrary"),
                                             vmem_limit_bytes=VMEM_LIMIT),
    )(q, hg, cmp_kv, kvs_b, kvw_b, smap)


def _attn_sample_kernel(pt_ref, q_ref, hg_ref, kvs_new_ref, kvw_new_ref, win_ref, cache_c_ref, cache_s_ref,
                        wp_ref, w1_ref, w2_ref, pe_ref, smap_ref,
                        o_ref,
                        buf_c, buf_s, win_tail, peb_ref, sem_c, sem_s, m_ref, l_ref, acc_ref,
                        *, layer, n_pages, page, past_len, t_new, tq, kc, n_cmp, n_sel, wb):
    b = pl.program_id(0)
    nb = pl.num_programs(0)
    slot = b % 2
    sel_rows_total = buf_s.shape[1]

    def page_copies(bb, sl):
        out = []
        for p in range(n_pages):
            pg = pt_ref[bb, p]
            for j in range(KV_ROW // PAIR):
                out.append(pltpu.make_async_copy(cache_c_ref.at[layer, pg, :, pl.ds(j * PAIR, PAIR)],
                                                 buf_c.at[sl, j, pl.ds(p * page, page)], sem_c.at[sl]))
            out.append(pltpu.make_async_copy(cache_s_ref.at[layer, pg], buf_s.at[sl, pl.ds(p * page, page)],
                                             sem_s.at[sl]))
        return out

    @pl.when(b == 0)
    def _():
        for cp in page_copies(0, 0):
            cp.start()
        for sl in range(2):
            buf_s[sl, past_len:sel_rows_total, :] = jnp.zeros((sel_rows_total - past_len, KV_ROW), F32)
        win_tail[...] = jnp.zeros_like(win_tail)
        for kv in range(2):
            peb_ref[kv] = jnp.broadcast_to(_pe_bias(pe_ref, w1_ref, kv), (8, CMP_HID))

    @pl.when(b + 1 < nb)
    def _():
        for cp in page_copies(b + 1, 1 - slot):
            cp.start()

    for cp in page_copies(b, slot):
        cp.wait()

    buf_s[slot, past_len:past_len + 8, :] = kvs_new_ref[0]
    win_tail[0:8, :] = kvw_new_ref[0]

    M = 8 * tq
    row = lax.broadcasted_iota(jnp.int32, (M, 1), 0)
    qpos = past_len + (row & (tq - 1))
    q = q_ref[0]
    hg = hg_ref[0]
    n_chunks = past_len // CMP_STRIDE
    n_win_chunks = wb // kc

    def setter(blk, val):
        o_ref[0, :, blk * LANES:(blk + 1) * LANES] = val.astype(o_ref.dtype)

    for pr in range(2):
        k_lane, v_lane = pr * PAIR, KV_DIM + pr * PAIR
        cmp2 = []
        for kv, lane0 in ((0, k_lane), (1, v_lane)):
            load = lambda pos, j=lane0 // PAIR: buf_c[slot, j, pl.ds(pos, n_chunks, stride=CMP_STRIDE), :]
            cmp2.append(_compress_pair(load, wp_ref, w2_ref, peb_ref[kv][0:1, :], kv, n_chunks).astype(BF16))
        rows = lambda c: pl.ds(c * kc, kc)

        def win_get(c, lane0):
            if c < n_win_chunks:
                return win_ref[0, rows(c), pl.ds(lane0, PAIR)].astype(BF16)
            return win_tail[:, pl.ds(lane0, PAIR)].astype(BF16)

        gated = _nsa_pair(
            q2=_build_q2(q, pr, tq), qpos=qpos, tq=tq, kc2=cmp2[0], vc2=cmp2[1],
            n_cmp=n_cmp, n_sel=n_sel, smap_ref=smap_ref, hg=hg, head0=2 * pr * GROUP,
            sel_chunks=list(range(sel_rows_total // kc)),
            sel_k=lambda c: buf_s[slot, rows(c), pl.ds(k_lane, PAIR)].astype(BF16),
            sel_v=lambda c: buf_s[slot, rows(c), pl.ds(v_lane, PAIR)].astype(BF16),
            win_chunks=list(range(n_win_chunks + 1)),
            win_k=lambda c: win_get(c, k_lane),
            win_v=lambda c: win_get(c, v_lane),
            win_base=past_len - wb, kc=kc, m_ref=m_ref, l_ref=l_ref, acc_ref=acc_ref)
        _store_heads(setter, gated, pr, tq)


def _attn_sample(page_table, q, hg, kvs_new, kvw_new, win_state, cache_c, cache_s, wp, w1, w2, pe, smap,
                 *, layer, t_new, kc):
    nb, n_pages = page_table.shape
    page = cache_c.shape[2]
    past_len = n_pages * page
    wb = win_state.shape[2]
    tq = q.shape[1]
    D = q.shape[2]
    L = past_len + t_new
    n_chunks = L // CMP_STRIDE
    assert past_len % CMP_STRIDE == 0 and n_chunks * CMP_STRIDE == past_len and n_chunks <= LANES
    assert past_len % kc == 0 and wb % kc == 0 and t_new <= tq == 8
    n_cmp = n_chunks - CMP_BLOCK // CMP_STRIDE + 1
    n_sel = -(-L // SEL_BLOCK)
    assert n_sel <= LANES
    sel_rows = past_len + kc
    M = 8 * tq
    grid_spec = pltpu.PrefetchScalarGridSpec(
        num_scalar_prefetch=1, grid=(nb,),
        in_specs=[pl.BlockSpec((1, tq, D), lambda b, pt: (b, 0, 0)),
                  pl.BlockSpec((1, tq, LANES), lambda b, pt: (b, 0, 0)),
                  pl.BlockSpec((1, tq, KV_ROW), lambda b, pt: (b, 0, 0)),
                  pl.BlockSpec((1, tq, KV_ROW), lambda b, pt: (b, 0, 0)),
                  pl.BlockSpec((None, 1, wb, KV_ROW), lambda b, pt: (layer, b, 0, 0)),
                  pl.BlockSpec(memory_space=pl.ANY),
                  pl.BlockSpec(memory_space=pl.ANY),
                  _const_spec(wp.shape), _const_spec(w1.shape), _const_spec(w2.shape), _const_spec(pe.shape),
                  _const_spec(smap.shape)],
        out_specs=pl.BlockSpec((1, tq, D), lambda b, pt: (b, 0, 0)),
        scratch_shapes=[pltpu.VMEM((2, KV_ROW // PAIR, past_len, PAIR), F32),
                        pltpu.VMEM((2, sel_rows, KV_ROW), F32),
                        pltpu.VMEM((kc, KV_ROW), F32),
                        pltpu.VMEM((2, 8, CMP_HID), F32),
                        pltpu.SemaphoreType.DMA((2,)), pltpu.SemaphoreType.DMA((2,)),
                        pltpu.VMEM((M, 1), F32), pltpu.VMEM((M, 1), F32), pltpu.VMEM((M, PAIR), F32)])
    return pl.pallas_call(
        functools.partial(_attn_sample_kernel, layer=layer, n_pages=n_pages, page=page, past_len=past_len,
                          t_new=t_new, tq=tq, kc=kc, n_cmp=n_cmp, n_sel=n_sel, wb=wb),
        grid_spec=grid_spec,
        out_shape=jax.ShapeDtypeStruct((nb, tq, D), BF16),
        name="attn_sample",
        compiler_params=pltpu.CompilerParams(dimension_semantics=("arbitrary",), vmem_limit_bytes=VMEM_LIMIT),
    )(page_table, q, hg, kvs_new, kvw_new, win_state, cache_c, cache_s, wp, w1, w2, pe, smap)


def _post_kernel(o_ref, yc_ref, ga_ref, gc_ref, x_ref, g_ref, wa_ref, wc_ref, wo_ref, w1_ref, w2_ref, out_ref,
                 *, ff_chunk):
    g = g_ref[...]
    a = _dot(o_ref[...], wa_ref[...])
    c = _dot(yc_ref[...].astype(BF16), wc_ref[...])
    mix = _dot((ga_ref[...] * a + gc_ref[...] * c).astype(BF16), wo_ref[...])
    x1 = x_ref[...] + _rms(mix, g[1:2])
    h2 = _rms(x1, g[2:3]).astype(BF16)
    f = jnp.zeros(x1.shape, F32)
    for j in range(w1_ref.shape[1] // ff_chunk):
        t = jnp.maximum(_dot(h2, w1_ref[:, j * ff_chunk:(j + 1) * ff_chunk]), 0.0)
        f = f + _dot((t * t).astype(BF16), w2_ref[j * ff_chunk:(j + 1) * ff_chunk, :])
    out_ref[...] = x1 + _rms(f, g[3:4])


def _post(o, yc, ga, gc, x, g, wa, wc, wo, w1, w2, *, tm):
    M, D = x.shape
    row_spec = pl.BlockSpec((tm, D), lambda i: (i, 0))
    return pl.pallas_call(
        functools.partial(_post_kernel, ff_chunk=1024),
        grid=(M // tm,),
        in_specs=[row_spec] * 5 + [_const_spec(a.shape) for a in (g, wa, wc, wo, w1, w2)],
        out_specs=row_spec,
        out_shape=jax.ShapeDtypeStruct((M, D), F32),
        name="post",
        compiler_params=pltpu.CompilerParams(dimension_semantics=("arbitrary",), vmem_limit_bytes=VMEM_LIMIT),
    )(o, yc, ga, gc, x, g, wa, wc, wo, w1, w2)


def _sel_map(n_chunks, n_cmp, n_sel):
    cs = np.arange(n_chunks)[:, None] * CMP_STRIDE
    ss = np.arange(LANES)[None, :] * SEL_BLOCK
    ov = np.clip(np.minimum(cs + CMP_BLOCK, ss + SEL_BLOCK) - np.maximum(cs, ss), 0, None) / CMP_STRIDE
    ov = ov * (np.arange(n_chunks)[:, None] < n_cmp) * (np.arange(LANES)[None, :] < n_sel)
    return jnp.asarray(ov, dtype=BF16)


def _prep_weights(w_in, cmp_pe, cmp_w1, cmp_w2):
    D = w_in.shape[1]
    assert D == _D
    kvd = 2 * KV_DIM
    o_q, o_c, o_s, o_w, o_h = 0, D, D + kvd, D + 2 * kvd, D + 3 * kvd
    o_b = o_h + 3 * N_HEADS
    o_cg, o_x, o_m = o_b + D, o_b + 2 * D, o_b + 3 * D
    sl = lambda a, n: w_in[:, :, a:a + n]
    hg_pad = jnp.pad(sl(o_h, 3 * N_HEADS), ((0, 0), (0, 0), (0, LANES - 3 * N_HEADS)))
    w_cat = jnp.concatenate([sl(o_q, D), sl(o_b, D), sl(o_cg, D), sl(o_x, D), sl(o_m, D), sl(o_m + D, D),
                             sl(o_c, kvd), sl(o_s, kvd), sl(o_w, kvd), hg_pad], axis=2).astype(BF16)
    depth = w_in.shape[0]
    half = CMP_STRIDE * HEAD_DIM
    w1r = cmp_w1.reshape(depth, 2, 2, CMP_STRIDE, HEAD_DIM, CMP_HID)
    ab = jnp.concatenate([w1r[:, :, 0], w1r[:, :, 1]], axis=-1)
    z = jnp.zeros_like(ab)
    wp = jnp.concatenate([jnp.concatenate([ab, z], axis=-1), jnp.concatenate([z, ab], axis=-1)], axis=-2)
    z2 = jnp.zeros_like(cmp_w2)
    w2bd = jnp.concatenate([jnp.concatenate([cmp_w2, z2], axis=-1), jnp.concatenate([z2, cmp_w2], axis=-1)], axis=-2)
    pe8 = jnp.broadcast_to(cmp_pe.reshape(depth, 2, 1, 2 * half), (depth, 2, 8, 2 * half))
    return w_cat, wp.astype(BF16), cmp_w1.astype(BF16), w2bd.astype(BF16), pe8.astype(BF16)


def kernel(x_prompt, x_sample, cache_cmp_kv, cache_sel_kv, state_win_kv, state_conv, page_table, norm_g, w_in,
           cmp_pe, cmp_w1, cmp_w2, conv_w, w_up_attn, w_up_conv, w_o, w_ff1, w_ff2):
    B, T, D = x_prompt.shape
    NB, TS, _ = x_sample.shape
    depth = w_in.shape[0]
    n_pool, page = cache_cmp_kv.shape[1:3]
    wb = state_win_kv.shape[2]
    past_len = page_table.shape[1] * page
    tq_s = 8

    w_cat, wp, w1b, w2bd, pe8 = _prep_weights(w_in, cmp_pe, cmp_w1, cmp_w2)
    wa, wc, wo, wf1, wf2 = (w.astype(BF16) for w in (w_up_attn, w_up_conv, w_o, w_ff1, w_ff2))
    cache_c = cache_cmp_kv.reshape(depth, n_pool, page, KV_ROW)
    cache_s = cache_sel_kv.reshape(depth, n_pool, page, KV_ROW)
    win_state = state_win_kv.reshape(depth, NB, wb, KV_ROW)
    zs = lambda n: jnp.zeros((depth, NB, n, D), F32)
    p1 = jnp.concatenate([state_conv[:, :, 1:2], zs(TS - 1)], axis=2).reshape(depth, NB * TS, D)
    p2 = jnp.concatenate([state_conv, zs(TS - 2)], axis=2).reshape(depth, NB * TS, D)

    n_chunks_p = T // CMP_STRIDE
    n_cmp_p = n_chunks_p - CMP_BLOCK // CMP_STRIDE + 1
    smap_p = _sel_map(n_chunks_p, n_cmp_p, -(-T // SEL_BLOCK))
    L = past_len + TS
    n_chunks_s = L // CMP_STRIDE
    smap_s = _sel_map(n_chunks_s, n_chunks_s - CMP_BLOCK // CMP_STRIDE + 1, -(-L // SEL_BLOCK))

    tm_p = min(256, T)
    tq_p = min(128, T)
    kc = 256
    pad_t = lambda a: jnp.pad(a.reshape(NB, TS, a.shape[-1]), ((0, 0), (0, tq_s - TS), (0, 0)))

    xp = x_prompt.reshape(B * T, D)
    xs = x_sample.reshape(NB * TS, D)
    outs = [[] for _ in range(8)]
    for l in range(depth):
        g = norm_g[l]
        (q, yc, ga, gc, kvc, kvs, kvw, kvs_b, kvw_b, hg, ust) = _proj(
            xp, g[0:1], w_cat[l], conv_w[l], tm=tm_p, seq_len=T)
        cmp_kv = _compress(kvc.reshape(B, T, KV_ROW), wp[l], w1b[l], w2bd[l], pe8[l])
        o = _attn_prompt(q, hg, cmp_kv, kvs_b.reshape(B, T, KV_ROW), kvw_b.reshape(B, T, KV_ROW), smap_p,
                         B=B, T=T, tq=tq_p, kc=min(kc, T))
        xp = _post(o, yc, ga, gc, xp, g, wa[l], wc[l], wo[l], wf1[l], wf2[l], tm=tm_p)
        kv5 = lambda a, n: a.reshape(n, -1, 2, N_KV, HEAD_DIM)
        outs[0].append(kv5(kvc, B))
        outs[2].append(kv5(kvs, B))
        outs[4].append(kv5(kvw, B)[:, -min(WINDOW, T):])
        outs[6].append(ust[:, 8 - (CONV_W - 1):])
        (q, yc, ga, gc, kvc, kvs, kvw, _, _, hg, u) = _proj(
            xs, g[0:1], w_cat[l], conv_w[l], tm=NB * TS, seq_len=TS, prefix=(p1[l], p2[l]))
        o = _attn_sample(page_table, pad_t(q), pad_t(hg), pad_t(kvs), pad_t(kvw), win_state, cache_c, cache_s,
                         wp[l], w1b[l], w2bd[l], pe8[l], smap_s, layer=l, t_new=TS, kc=kc)
        o = o[:, :TS].reshape(NB * TS, D)
        xs = _post(o, yc, ga, gc, xs, g, wa[l], wc[l], wo[l], wf1[l], wf2[l], tm=min(256, NB * TS))
        outs[1].append(kv5(kvc, NB))
        outs[3].append(kv5(kvs, NB))
        win_all = jnp.concatenate([win_state[l], kvw.reshape(NB, TS, KV_ROW)], axis=1)[:, -wb:]
        outs[5].append(win_all.reshape(NB, wb, 2, N_KV, HEAD_DIM))
        outs[7].append(u.reshape(NB, TS, D)[:, TS - (CONV_W - 1):])
    st = [jnp.stack(o) for o in outs]
    return (xp.reshape(B, T, D), xs.reshape(NB, TS, D), st[0], st[1], st[2], st[3], st[4], st[5], st[6], st[7])
```

```python
import functools

import numpy as np
import jax
import jax.numpy as jnp
from jax import lax
from jax.experimental import pallas as pl
from jax.experimental.pallas import tpu as pltpu

F32 = jnp.float32
BF16 = jnp.bfloat16

HEAD_DIM = 64
N_KV = 4
GROUP = 4
N_HEADS = N_KV * GROUP
KV_DIM = N_KV * HEAD_DIM
KV_ROW = 2 * KV_DIM
PAIR = 2 * HEAD_DIM
CMP_BLOCK = 32
CMP_STRIDE = 16
CMP_HID = 2 * HEAD_DIM
SEL_BLOCK = 64
TOP_N = 16
WINDOW = 512
CONV_W = 3
EPS = 1e-6
NEG = -1e30
BIG = 1e30
LOG2E = 1.4426950408889634
FORCE_BONUS = 1e4
LANES = 128
VMEM_LIMIT = 56 * 1024 * 1024


def _rms(x, g):
    return x * lax.rsqrt(jnp.mean(x * x, axis=-1, keepdims=True) + EPS) * g


def _sigmoid(x):
    return 1.0 / (1.0 + jnp.exp(-x))


def _dot(a, b):
    return jnp.dot(a, b, preferred_element_type=F32)


def _dot_nt(a, b):
    return lax.dot_general(a, b, (((1,), (1,)), ((), ())), preferred_element_type=F32)


def _const_spec(shape):
    nd = len(shape)
    return pl.BlockSpec(shape, lambda *_: (0,) * nd, pipeline_mode=pl.Buffered(1))


_D = 1024
_OFF_Q, _OFF_B, _OFF_C, _OFF_X, _OFF_GA, _OFF_GC = 0, _D, 2 * _D, 3 * _D, 4 * _D, 5 * _D
_OFF_KVC = 6 * _D
_OFF_KVS = _OFF_KVC + KV_ROW
_OFF_KVW = _OFF_KVS + KV_ROW
_OFF_HG = _OFF_KVW + KV_ROW
_NP = _OFF_HG + LANES


def _proj_kernel(*refs, carry_mode, tiles_per_seq, seq_len, q_scale):
    if carry_mode:
        (x_ref, g_ref, w_ref, cw_ref,
         q_ref, yc_ref, ga_ref, gc_ref, kvc_ref, kvs_ref, kvw_ref, ksb_ref, vst_ref, kwb_ref, vwt_ref, hg_ref, ust_ref,
         carry_ref) = refs
    else:
        (x_ref, g_ref, w_ref, cw_ref, p1_ref, p2_ref,
         q_ref, yc_ref, ga_ref, gc_ref, kvc_ref, kvs_ref, kvw_ref, kvsb_ref, kvwb_ref, hg_ref, ust_ref) = refs
    tm = x_ref.shape[0]
    h = _rms(x_ref[...], g_ref[...]).astype(BF16)

    def mm(off, width):
        return _dot(h, w_ref[:, off:off + width])

    q_ref[...] = (mm(_OFF_Q, _D) * q_scale).astype(BF16)
    u = mm(_OFF_C, _D) * mm(_OFF_X, _D)
    row = lax.broadcasted_iota(jnp.int32, (tm, 1), 0)
    r1 = pltpu.roll(u, 1, axis=0)
    r2 = pltpu.roll(u, 2, axis=0)
    if carry_mode:
        @pl.when(pl.program_id(0) % tiles_per_seq == 0)
        def _():
            carry_ref[...] = jnp.zeros_like(carry_ref)
        c6 = carry_ref[6:7, :]
        c7 = carry_ref[7:8, :]
        u_m1 = jnp.where(row == 0, c7, r1)
        u_m2 = jnp.where(row == 0, c6, jnp.where(row == 1, c7, r2))
        tail = u[tm - 8:tm, :]
        carry_ref[...] = tail
        ust_ref[0] = tail
    else:
        t_in = row & (seq_len - 1)
        u_m1 = jnp.where(t_in == 0, p1_ref[...], r1)
        u_m2 = jnp.where(t_in < 2, p2_ref[...], r2)
        ust_ref[...] = u
    cw = cw_ref[...]
    yc_ref[...] = mm(_OFF_B, _D) * (cw[0:1] * u_m2 + cw[1:2] * u_m1 + cw[2:3] * u)
    ga_ref[...] = _sigmoid(mm(_OFF_GA, _D))
    gc_ref[...] = _sigmoid(mm(_OFF_GC, _D))
    kvc_ref[...] = mm(_OFF_KVC, KV_ROW)
    kvs = mm(_OFF_KVS, KV_ROW)
    kvs_ref[...] = kvs
    kvw = mm(_OFF_KVW, KV_ROW)
    kvw_ref[...] = kvw
    if carry_mode:
        ksb_ref[...] = kvs[:, :KV_DIM].astype(BF16)
        vst_ref[0] = kvs[:, KV_DIM:].T.astype(BF16)
        kwb_ref[...] = kvw[:, :KV_DIM].astype(BF16)
        vwt_ref[0] = kvw[:, KV_DIM:].T.astype(BF16)
    else:
        kvsb_ref[...] = kvs.astype(BF16)
        kvwb_ref[...] = kvw.astype(BF16)
    hg_ref[...] = _sigmoid(mm(_OFF_HG, LANES))


def _proj(x, g, w, cw, *, tm, seq_len, q_scale, prefix=None):
    M, D = x.shape
    carry_mode = prefix is None
    n_tiles = M // tm
    row_spec = lambda width: pl.BlockSpec((tm, width), lambda i: (i, 0))
    in_specs = [row_spec(D), _const_spec((1, D)), _const_spec((D, _NP)), _const_spec((CONV_W, D))]
    args = [x, g, w, cw]
    scratch = []
    if carry_mode:
        assert seq_len % tm == 0
        tiles_per_seq = seq_len // tm
        n_seq = M // seq_len
        ust_shape = jax.ShapeDtypeStruct((n_seq, 8, D), F32)
        ust_spec = pl.BlockSpec((1, 8, D), lambda i: (i // tiles_per_seq, 0, 0))
        scratch = [pltpu.VMEM((8, D), F32)]
        vt_shape = jax.ShapeDtypeStruct((n_seq, KV_DIM, seq_len), BF16)
        vt_spec = pl.BlockSpec((1, KV_DIM, tm), lambda i: (i // tiles_per_seq, 0, i % tiles_per_seq))
        kv_copies = [(jax.ShapeDtypeStruct((M, KV_DIM), BF16), row_spec(KV_DIM)), (vt_shape, vt_spec)] * 2
    else:
        assert tm % seq_len == 0 and seq_len >= CONV_W - 1
        tiles_per_seq = 1
        in_specs += [row_spec(D), row_spec(D)]
        args += list(prefix)
        ust_shape = jax.ShapeDtypeStruct((M, D), F32)
        ust_spec = row_spec(D)
        kv_copies = [(jax.ShapeDtypeStruct((M, KV_ROW), BF16), row_spec(KV_ROW))] * 2
    outs = ([(jax.ShapeDtypeStruct((M, D), BF16), row_spec(D))]
            + [(jax.ShapeDtypeStruct((M, D), F32), row_spec(D))] * 3
            + [(jax.ShapeDtypeStruct((M, KV_ROW), F32), row_spec(KV_ROW))] * 3
            + kv_copies
            + [(jax.ShapeDtypeStruct((M, LANES), F32), row_spec(LANES)),
               (ust_shape, ust_spec)])
    out_shape = [o[0] for o in outs]
    out_specs = [o[1] for o in outs]
    return pl.pallas_call(
        functools.partial(_proj_kernel, carry_mode=carry_mode, tiles_per_seq=tiles_per_seq, seq_len=seq_len,
                          q_scale=q_scale),
        grid=(n_tiles,), in_specs=in_specs, out_specs=out_specs, out_shape=out_shape,
        scratch_shapes=scratch, name="proj_carry" if carry_mode else "proj_prefix",
        compiler_params=pltpu.CompilerParams(dimension_semantics=("arbitrary",), vmem_limit_bytes=VMEM_LIMIT),
    )(*args)


def _pe_bias(pe_ref, w1_ref, kv):
    return _dot(pe_ref[kv], w1_ref[kv])[0:1, :]


def _compress_pair(load_rows, wp_ref, w2_ref, peb, kv, n_chunks):
    acc = jnp.zeros((n_chunks, 4 * CMP_HID), F32)
    for pos in range(CMP_STRIDE):
        acc = acc + _dot(load_rows(pos).astype(BF16), wp_ref[kv, pos])
    hid = []
    for gl in range(2):
        a = acc[:, gl * 2 * CMP_HID: gl * 2 * CMP_HID + CMP_HID]
        b = acc[:, gl * 2 * CMP_HID + CMP_HID: (gl + 1) * 2 * CMP_HID]
        hid.append(jnp.maximum(a + pltpu.roll(b, n_chunks - 1, axis=0) + peb, 0.0))
    return _dot(jnp.concatenate(hid, axis=1).astype(BF16), w2_ref[kv])


def _compress_kernel(k_ref, v_ref, wp_ref, w1_ref, w2_ref, pe_ref, kc_ref, vct_ref, *, n_chunks):
    for kv, src_ref in ((0, k_ref), (1, v_ref)):
        load = lambda pos, src_ref=src_ref: src_ref[0, pl.ds(pos, n_chunks, stride=CMP_STRIDE), :]
        peb = _pe_bias(pe_ref, w1_ref, kv)
        out = _compress_pair(load, wp_ref, w2_ref, peb, kv, n_chunks)
        if kv == 0:
            kc_ref[0] = out.astype(BF16)
        else:
            vct_ref[0] = out.T.astype(BF16)


def _compress(kvc, wp, w1, w2, pe):
    B, T, _ = kvc.shape
    n_chunks = T // CMP_STRIDE
    n_pairs = KV_DIM // PAIR
    return pl.pallas_call(
        functools.partial(_compress_kernel, n_chunks=n_chunks),
        grid=(B, n_pairs),
        in_specs=[pl.BlockSpec((1, T, PAIR), lambda b, j: (b, 0, j)),
                  pl.BlockSpec((1, T, PAIR), lambda b, j: (b, 0, n_pairs + j)),
                  _const_spec(wp.shape), _const_spec(w1.shape), _const_spec(w2.shape), _const_spec(pe.shape)],
        out_specs=[pl.BlockSpec((1, n_chunks, PAIR), lambda b, j: (b, 0, j)),
                   pl.BlockSpec((1, PAIR, n_chunks), lambda b, j: (b, j, 0))],
        out_shape=[jax.ShapeDtypeStruct((B, n_chunks, KV_DIM), BF16),
                   jax.ShapeDtypeStruct((B, KV_DIM, n_chunks), BF16)],
        name="compress",
        compiler_params=pltpu.CompilerParams(dimension_semantics=("arbitrary", "arbitrary"),
                                             vmem_limit_bytes=VMEM_LIMIT),
    )(kvc, kvc, wp, w1, w2, pe)


def _build_q2(q, pr, tq):
    zero = jnp.zeros((tq, HEAD_DIM), F32)
    g0, g1 = [], []
    for r in range(GROUP):
        h0 = (2 * pr) * GROUP + r
        h1 = (2 * pr + 1) * GROUP + r
        g0.append(jnp.concatenate([q[:, h0 * HEAD_DIM:(h0 + 1) * HEAD_DIM].astype(F32), zero], axis=1))
        g1.append(jnp.concatenate([zero, q[:, h1 * HEAD_DIM:(h1 + 1) * HEAD_DIM].astype(F32)], axis=1))
    return jnp.concatenate(g0 + g1, axis=0).astype(BF16)


class _Softmax:
    def __init__(self, m_ref, l_ref, acc_ref):
        self.m_ref, self.l_ref, self.acc_ref = m_ref, l_ref, acc_ref

    def reset(self):
        self.m_ref[...] = jnp.full(self.m_ref.shape, NEG, F32)
        self.l_ref[...] = jnp.zeros(self.l_ref.shape, F32)
        self.acc_ref[...] = jnp.zeros(self.acc_ref.shape, F32)

    def update(self, s, vt):
        m_old = self.m_ref[...]
        m_new = jnp.maximum(m_old, jnp.max(s, axis=0, keepdims=True))
        alpha = jnp.exp2(m_old - m_new)
        p = jnp.exp2(s - m_new)
        self.l_ref[...] = alpha * self.l_ref[...] + jnp.sum(p, axis=0, keepdims=True)
        self.acc_ref[...] = alpha * self.acc_ref[...] + _dot(vt, p.astype(BF16))
        self.m_ref[...] = m_new

    def result(self):
        return self.acc_ref[...] * (1.0 / jnp.maximum(self.l_ref[...], 1e-30))


def _attn_prompt_kernel(q_ref, hg_ref, kc_ref, vct_ref, ks_ref, vst_ref, kw_ref, vwt_ref, e_ref, smapt_ref,
                        o_ref, m_ref, l_ref, acc_ref, *, tq, kc, n_cmp, n_sel):
    i = pl.program_id(1)
    q0 = i * tq
    M = 2 * GROUP * tq
    col = lax.broadcasted_iota(jnp.int32, (1, M), 1)
    qpos = q0 + (col & (tq - 1))
    qpos_t = qpos[:, 0:tq]
    q = q_ref[...]
    hg_t = hg_ref[...].T
    n_chunks = kc_ref.shape[1]
    nsp = smapt_ref.shape[0]
    n_top = min(TOP_N, n_sel)
    n_row = lax.broadcasted_iota(jnp.int32, (n_chunks, 1), 0)
    j_row = lax.broadcasted_iota(jnp.int32, (nsp, 1), 0)
    sm = _Softmax(m_ref, l_ref, acc_ref)
    win_steps = WINDOW // tq

    for pr in range(KV_DIM // PAIR):
        lanes = pl.ds(pr * PAIR, PAIR)
        q2 = _build_q2(q, pr, tq)

        s = _dot_nt(kc_ref[0, :, lanes], q2)
        cmask = (n_row * CMP_STRIDE + (CMP_BLOCK - 1) <= qpos) & (n_row < n_cmp)
        s = jnp.where(cmask, s, NEG)
        e = jnp.where(cmask, jnp.exp2(s - jnp.max(s, axis=0, keepdims=True)), 0.0)
        p_cmp = e * (1.0 / jnp.maximum(jnp.sum(e, axis=0, keepdims=True), 1e-30))
        o_cmp = _dot(vct_ref[0, lanes, :], p_cmp.astype(BF16))

        cur = qpos_t >> 6
        forced = (j_row == 0) | (j_row == cur) | (j_row == cur - 1)
        causal_blk = j_row * SEL_BLOCK <= qpos_t
        bias_rows = []
        for gl in range(2):
            psum = p_cmp[:, (gl * GROUP) * tq:(gl * GROUP + 1) * tq]
            for r in range(1, GROUP):
                psum = psum + p_cmp[:, (gl * GROUP + r) * tq:(gl * GROUP + r + 1) * tq]
            p_hi = psum.astype(BF16)
            p_lo = (psum - p_hi.astype(F32)).astype(BF16)
            imp = _dot(smapt_ref[...], p_hi) + _dot(smapt_ref[...], p_lo)
            imp = jnp.where(causal_blk, imp + jnp.where(forced, FORCE_BONUS, 0.0), NEG)
            imp = jnp.where(j_row < n_sel, imp, -jnp.inf)
            cnt = jnp.zeros((nsp, tq), F32)
            for jb in range(n_sel):
                ri = imp[jb:jb + 1, :]
                tie = jnp.where(j_row > jb, 1.0, 0.0)
                cnt = cnt + jnp.where(ri > imp, 1.0, jnp.where(ri == imp, tie, 0.0))
            bias_t = jnp.where(cnt < n_top, 0.0, -BIG)
            bias_t = jnp.concatenate([bias_t, jnp.zeros((LANES - nsp, tq), F32)], axis=0)
            bias_rows += [bias_t.T.astype(BF16)] * GROUP
        rhs_sel = jnp.concatenate([q2, jnp.concatenate(bias_rows, axis=0)], axis=1)

        sel_rows = lambda c: pl.ds(pl.multiple_of(c * kc, kc), kc)

        def sel_scores(c):
            lhs = jnp.concatenate([ks_ref[0, sel_rows(c), lanes], e_ref[sel_rows(c), :]], axis=1)
            return _dot_nt(lhs, rhs_sel)

        sm.reset()
        c_diag = (q0 + tq - 1) // kc

        def sel_body(c, s_cur):
            s_next = sel_scores(c + 1)
            sm.update(s_cur, vst_ref[0, lanes, sel_rows(c)])
            return s_next
        s = lax.fori_loop(0, c_diag, sel_body, sel_scores(0))
        kpos = c_diag * kc + lax.broadcasted_iota(jnp.int32, (kc, 1), 0)
        sm.update(jnp.where(kpos <= qpos, s, NEG), vst_ref[0, lanes, sel_rows(c_diag)])
        o_sel = sm.result()

        win_rows = lambda c: pl.ds(pl.multiple_of(c * tq, tq), tq)
        win_scores = lambda c: _dot_nt(kw_ref[0, win_rows(c), lanes], q2)

        sm.reset()
        k_row = lax.broadcasted_iota(jnp.int32, (tq, 1), 0)

        @pl.when(i >= win_steps)
        def _():
            c = i - win_steps
            sm.update(jnp.where(q0 - WINDOW + k_row > qpos - WINDOW, win_scores(c), NEG), vwt_ref[0, lanes, win_rows(c)])

        def win_body(c, s_cur):
            s_next = win_scores(c + 1)
            sm.update(s_cur, vwt_ref[0, lanes, win_rows(c)])
            return s_next
        c_first = jnp.maximum(i - win_steps + 1, 0)
        s = lax.fori_loop(c_first, i, win_body, win_scores(c_first))
        sm.update(jnp.where(q0 + k_row <= qpos, s, NEG), vwt_ref[0, lanes, win_rows(i)])
        o_win = sm.result()

        def gate_row(br):
            head0 = 2 * pr * GROUP
            return jnp.concatenate([hg_t[(head0 + j) * 3 + br:(head0 + j) * 3 + br + 1, :]
                                    for j in range(2 * GROUP)], axis=1)
        gated = gate_row(0) * o_cmp + gate_row(1) * o_sel + gate_row(2) * o_win
        for gl in range(2):
            for half in range(GROUP // 2):
                c0 = (gl * GROUP + 2 * half) * tq
                two_heads = jnp.concatenate([gated[gl * HEAD_DIM:(gl + 1) * HEAD_DIM, c0:c0 + tq],
                                             gated[gl * HEAD_DIM:(gl + 1) * HEAD_DIM, c0 + tq:c0 + 2 * tq]], axis=0)
                blk = (2 * pr + gl) * (GROUP // 2) + half
                o_ref[:, blk * LANES:(blk + 1) * LANES] = two_heads.T.astype(o_ref.dtype)


def _attn_prompt(q, hg, kcmp, vcmp_t, ks, vs_t, kw, vw_t, e_blk, smap_t, *, B, T, tq, kc):
    n_t = T // tq
    n_chunks = T // CMP_STRIDE
    n_cmp = n_chunks - CMP_BLOCK // CMP_STRIDE + 1
    n_sel = -(-T // SEL_BLOCK)
    assert T % kc == 0 and T % tq == 0 and kc % tq == 0 and WINDOW % tq == 0 and tq == LANES
    assert n_sel <= LANES and n_chunks <= LANES
    M = 2 * GROUP * tq
    D = q.shape[1]
    seq_rows = lambda width: pl.BlockSpec((1, T, width), lambda b, i: (b, 0, 0))
    seq_cols = pl.BlockSpec((1, KV_DIM, T), lambda b, i: (b, 0, 0))
    return pl.pallas_call(
        functools.partial(_attn_prompt_kernel, tq=tq, kc=kc, n_cmp=n_cmp, n_sel=n_sel),
        grid=(B, n_t),
        in_specs=[pl.BlockSpec((tq, D), lambda b, i: (b * n_t + i, 0)),
                  pl.BlockSpec((tq, LANES), lambda b, i: (b * n_t + i, 0)),
                  pl.BlockSpec((1, n_chunks, KV_DIM), lambda b, i: (b, 0, 0)),
                  pl.BlockSpec((1, KV_DIM, n_chunks), lambda b, i: (b, 0, 0)),
                  seq_rows(KV_DIM), seq_cols, seq_rows(KV_DIM), seq_cols,
                  _const_spec(e_blk.shape), _const_spec(smap_t.shape)],
        out_specs=pl.BlockSpec((tq, D), lambda b, i: (b * n_t + i, 0)),
        out_shape=jax.ShapeDtypeStruct((B * T, D), BF16),
        scratch_shapes=[pltpu.VMEM((1, M), F32), pltpu.VMEM((1, M), F32), pltpu.VMEM((PAIR, M), F32)],
        name="attn_prompt",
        compiler_params=pltpu.CompilerParams(dimension_semantics=("arbitrary", "arbitrary"),
                                             vmem_limit_bytes=VMEM_LIMIT),
    )(q, hg, kcmp, vcmp_t, ks, vs_t, kw, vw_t, e_blk, smap_t)


def _softmax_rows(parts):
    m = functools.reduce(jnp.maximum, [jnp.max(s, axis=1, keepdims=True) for s in parts])
    ps = [jnp.exp2(s - m) for s in parts]
    inv = 1.0 / jnp.maximum(functools.reduce(jnp.add, [jnp.sum(p, axis=1, keepdims=True) for p in ps]), 1e-30)
    return [p * inv for p in ps]


def _attn_sample_kernel(pt_ref, q_ref, hg_ref, kvs_new_ref, kvw_new_ref, win_ref, cache_c_ref, cache_s_ref,
                        wp_ref, w1_ref, w2_ref, pe_ref, smap_ref, e_ref,
                        o_ref,
                        buf_c, buf_s, peb_ref, sem_c, sem_s,
                        *, layer, n_pages, page, past_len, tq, n_cmp, n_sel, wb):
    b = pl.program_id(0)
    nb = pl.num_programs(0)
    slot = b % 2

    def page_copies(bb, sl):
        out = []
        for p in range(n_pages):
            pg = pt_ref[bb, p]
            for j in range(KV_ROW // PAIR):
                out.append(pltpu.make_async_copy(cache_c_ref.at[layer, pg, :, pl.ds(j * PAIR, PAIR)],
                                                 buf_c.at[sl, j, pl.ds(p * page, page)], sem_c.at[sl]))
            out.append(pltpu.make_async_copy(cache_s_ref.at[layer, pg], buf_s.at[sl, pl.ds(p * page, page)],
                                             sem_s.at[sl]))
        return out

    @pl.when(b == 0)
    def _():
        for cp in page_copies(0, 0):
            cp.start()
        for kv in range(2):
            peb_ref[kv] = jnp.broadcast_to(_pe_bias(pe_ref, w1_ref, kv), (8, CMP_HID))

    @pl.when(b + 1 < nb)
    def _():
        for cp in page_copies(b + 1, 1 - slot):
            cp.start()

    for cp in page_copies(b, slot):
        cp.wait()

    M = N_HEADS * tq
    n_new = LANES
    pad_new = lambda ref: jnp.concatenate([ref[0], jnp.zeros((n_new - ref.shape[1], KV_ROW), BF16)], axis=0)
    kvs_new = pad_new(kvs_new_ref)
    kvw_new = pad_new(kvw_new_ref)
    row = lax.broadcasted_iota(jnp.int32, (M, 1), 0)
    qpos = past_len + (row & (tq - 1))
    lane = lax.broadcasted_iota(jnp.int32, (1, LANES), 1)
    q = q_ref[0].astype(F32)
    hg = hg_ref[0]
    n_chunks = past_len // CMP_STRIDE

    blocks = []
    for h in range(N_HEADS):
        g = h // GROUP
        pieces = [jnp.zeros((tq, HEAD_DIM * g), F32)] if g else []
        pieces.append(q[:, h * HEAD_DIM:(h + 1) * HEAD_DIM])
        if g < N_KV - 1:
            pieces.append(jnp.zeros((tq, HEAD_DIM * (N_KV - 1 - g)), F32))
        blocks.append(jnp.concatenate(pieces, axis=1) if len(pieces) > 1 else pieces[0])
    q4 = jnp.concatenate(blocks, axis=0).astype(BF16)

    cmp_kv = []
    for kv in range(2):
        halves = []
        for pr in range(KV_DIM // PAIR):
            load = lambda pos, j=kv * (KV_DIM // PAIR) + pr: buf_c[slot, j, pl.ds(pos, n_chunks, stride=CMP_STRIDE), :]
            halves.append(_compress_pair(load, wp_ref, w2_ref, peb_ref[kv][0:1, :], kv, n_chunks))
        cmp_kv.append(jnp.concatenate(halves, axis=1).astype(BF16))

    s = _dot_nt(q4, cmp_kv[0])
    n_idx = lax.broadcasted_iota(jnp.int32, (1, n_chunks), 1)
    cmask = (n_idx * CMP_STRIDE + (CMP_BLOCK - 1) <= qpos) & (n_idx < n_cmp)
    s = jnp.where(cmask, s, NEG)
    e = jnp.where(cmask, jnp.exp2(s - jnp.max(s, axis=1, keepdims=True)), 0.0)
    p_cmp = e * (1.0 / jnp.maximum(jnp.sum(e, axis=1, keepdims=True), 1e-30))
    o_cmp = _dot(p_cmp.astype(BF16), cmp_kv[1])

    psum = []
    for g in range(N_KV):
        acc = p_cmp[(g * GROUP) * tq:(g * GROUP + 1) * tq, :]
        for r in range(1, GROUP):
            acc = acc + p_cmp[(g * GROUP + r) * tq:(g * GROUP + r + 1) * tq, :]
        psum.append(acc)
    psum = jnp.concatenate(psum, axis=0)
    p_hi = psum.astype(BF16)
    p_lo = (psum - p_hi.astype(F32)).astype(BF16)
    imp = _dot(p_hi, smap_ref[...]) + _dot(p_lo, smap_ref[...])
    qpos_g = qpos[0:N_KV * tq, :]
    cur = qpos_g >> 6
    forced = (lane == 0) | (lane == cur) | (lane == cur - 1)
    imp = jnp.where(lane * SEL_BLOCK <= qpos_g, imp + jnp.where(forced, FORCE_BONUS, 0.0), NEG)
    imp = jnp.where(lane < n_sel, imp, -jnp.inf)
    cnt = jnp.zeros(imp.shape, F32)
    for jb in range(n_sel):
        ci = imp[:, jb:jb + 1]
        tie = jnp.where(lane > jb, 1.0, 0.0)
        cnt = cnt + jnp.where(ci > imp, 1.0, jnp.where(ci == imp, tie, 0.0))
    neg = jnp.where(cnt < min(TOP_N, n_sel), 0.0, -BIG)
    neg = jnp.concatenate([neg[g * tq:(g + 1) * tq, :] for g in range(N_KV) for _ in range(GROUP)], axis=0)
    bias = _dot(neg.astype(BF16), e_ref[...])

    new_pos = past_len + lax.broadcasted_iota(jnp.int32, (1, n_new), 1)
    s_old = _dot_nt(q4, buf_s[slot, :, 0:KV_DIM]) + bias[:, 0:past_len]
    s_new = _dot_nt(q4, kvs_new[:, 0:KV_DIM]) + bias[:, past_len:past_len + n_new]
    p_old, p_new = _softmax_rows([s_old, jnp.where(new_pos <= qpos, s_new, NEG)])
    o_sel = (_dot(p_old.astype(BF16), buf_s[slot, :, KV_DIM:KV_ROW])
             + _dot(p_new.astype(BF16), kvs_new[:, KV_DIM:KV_ROW]))

    st_pos = past_len - wb + lax.broadcasted_iota(jnp.int32, (1, wb), 1)
    s_st = jnp.where(st_pos > qpos - WINDOW, _dot_nt(q4, win_ref[0, :, 0:KV_DIM]), NEG)
    s_nw = jnp.where(new_pos <= qpos, _dot_nt(q4, kvw_new[:, 0:KV_DIM]), NEG)
    p_st, p_nw = _softmax_rows([s_st, s_nw])
    o_win = (_dot(p_st.astype(BF16), win_ref[0, :, KV_DIM:KV_ROW])
             + _dot(p_nw.astype(BF16), kvw_new[:, KV_DIM:KV_ROW]))

    gate = lambda br: jnp.concatenate([hg[:, h * 3 + br:h * 3 + br + 1] for h in range(N_HEADS)], axis=0)
    gated = gate(0) * o_cmp + gate(1) * o_sel + gate(2) * o_win

    for blk in range(N_HEADS // 2):
        g = (2 * blk) // GROUP
        lanes = slice((g // 2) * PAIR, (g // 2 + 1) * PAIR)
        even = gated[(2 * blk) * tq:(2 * blk + 1) * tq, lanes]
        odd = gated[(2 * blk + 1) * tq:(2 * blk + 2) * tq, lanes]
        if g % 2 == 0:
            odd = pltpu.roll(odd, HEAD_DIM, axis=1)
        else:
            even = pltpu.roll(even, HEAD_DIM, axis=1)
        o_ref[0, :, blk * LANES:(blk + 1) * LANES] = jnp.where(lane < HEAD_DIM, even, odd).astype(o_ref.dtype)


def _attn_sample(page_table, q, hg, kvs_new, kvw_new, win_state, cache_c, cache_s, wp, w1, w2, pe, smap, e_blk,
                 *, layer, t_new):
    nb, n_pages = page_table.shape
    page = cache_c.shape[2]
    past_len = n_pages * page
    wb = win_state.shape[2]
    tq = q.shape[1]
    n_new = kvs_new.shape[1]
    D = q.shape[2]
    L = past_len + t_new
    n_chunks = L // CMP_STRIDE
    assert past_len % CMP_STRIDE == 0 and n_chunks * CMP_STRIDE == past_len and n_chunks <= LANES
    assert t_new <= tq == 8 and t_new <= n_new <= LANES and wb <= WINDOW and e_blk.shape == (LANES, past_len + LANES)
    n_cmp = n_chunks - CMP_BLOCK // CMP_STRIDE + 1
    n_sel = -(-L // SEL_BLOCK)
    assert n_sel <= LANES
    grid_spec = pltpu.PrefetchScalarGridSpec(
        num_scalar_prefetch=1, grid=(nb,),
        in_specs=[pl.BlockSpec((1, tq, D), lambda b, pt: (b, 0, 0)),
                  pl.BlockSpec((1, tq, LANES), lambda b, pt: (b, 0, 0)),
                  pl.BlockSpec((1, n_new, KV_ROW), lambda b, pt: (b, 0, 0)),
                  pl.BlockSpec((1, n_new, KV_ROW), lambda b, pt: (b, 0, 0)),
                  pl.BlockSpec((None, 1, wb, KV_ROW), lambda b, pt: (layer, b, 0, 0)),
                  pl.BlockSpec(memory_space=pl.ANY),
                  pl.BlockSpec(memory_space=pl.ANY),
                  _const_spec(wp.shape), _const_spec(w1.shape), _const_spec(w2.shape), _const_spec(pe.shape),
                  _const_spec(smap.shape), _const_spec(e_blk.shape)],
        out_specs=pl.BlockSpec((1, tq, D), lambda b, pt: (b, 0, 0)),
        scratch_shapes=[pltpu.VMEM((2, KV_ROW // PAIR, past_len, PAIR), F32),
                        pltpu.VMEM((2, past_len, KV_ROW), BF16),
                        pltpu.VMEM((2, 8, CMP_HID), F32),
                        pltpu.SemaphoreType.DMA((2,)), pltpu.SemaphoreType.DMA((2,))])
    return pl.pallas_call(
        functools.partial(_attn_sample_kernel, layer=layer, n_pages=n_pages, page=page, past_len=past_len,
                          tq=tq, n_cmp=n_cmp, n_sel=n_sel, wb=wb),
        grid_spec=grid_spec,
        out_shape=jax.ShapeDtypeStruct((nb, tq, D), BF16),
        name="attn_sample",
        compiler_params=pltpu.CompilerParams(dimension_semantics=("arbitrary",), vmem_limit_bytes=VMEM_LIMIT),
    )(page_table, q, hg, kvs_new, kvw_new, win_state, cache_c, cache_s, wp, w1, w2, pe, smap, e_blk)


def _post_kernel(o_ref, yc_ref, ga_ref, gc_ref, x_ref, g_ref, wa_ref, wc_ref, wo_ref, w1_ref, w2_ref, out_ref,
                 *, ff_chunk):
    g = g_ref[...]
    a = _dot(o_ref[...], wa_ref[...])
    c = _dot(yc_ref[...].astype(BF16), wc_ref[...])
    mix = _dot((ga_ref[...] * a + gc_ref[...] * c).astype(BF16), wo_ref[...])
    x1 = x_ref[...] + _rms(mix, g[1:2])
    h2 = _rms(x1, g[2:3]).astype(BF16)
    f = jnp.zeros(x1.shape, F32)
    for j in range(w1_ref.shape[1] // ff_chunk):
        t = jnp.maximum(_dot(h2, w1_ref[:, j * ff_chunk:(j + 1) * ff_chunk]), 0.0)
        f = f + _dot((t * t).astype(BF16), w2_ref[j * ff_chunk:(j + 1) * ff_chunk, :])
    out_ref[...] = x1 + _rms(f, g[3:4])


def _post(o, yc, ga, gc, x, g, wa, wc, wo, w1, w2, *, tm):
    M, D = x.shape
    row_spec = pl.BlockSpec((tm, D), lambda i: (i, 0))
    return pl.pallas_call(
        functools.partial(_post_kernel, ff_chunk=1024),
        grid=(M // tm,),
        in_specs=[row_spec] * 5 + [_const_spec(a.shape) for a in (g, wa, wc, wo, w1, w2)],
        out_specs=row_spec,
        out_shape=jax.ShapeDtypeStruct((M, D), F32),
        name="post",
        compiler_params=pltpu.CompilerParams(dimension_semantics=("arbitrary",), vmem_limit_bytes=VMEM_LIMIT),
    )(o, yc, ga, gc, x, g, wa, wc, wo, w1, w2)


def _sel_map(n_chunks, n_cmp, n_sel):
    cs = np.arange(n_chunks)[:, None] * CMP_STRIDE
    ss = np.arange(LANES)[None, :] * SEL_BLOCK
    ov = np.clip(np.minimum(cs + CMP_BLOCK, ss + SEL_BLOCK) - np.maximum(cs, ss), 0, None) / CMP_STRIDE
    ov = ov * (np.arange(n_chunks)[:, None] < n_cmp) * (np.arange(LANES)[None, :] < n_sel)
    return jnp.asarray(ov, dtype=BF16)


def _prep_weights(w_in, cmp_pe, cmp_w1, cmp_w2):
    D = w_in.shape[1]
    assert D == _D
    kvd = 2 * KV_DIM
    o_q, o_c, o_s, o_w, o_h = 0, D, D + kvd, D + 2 * kvd, D + 3 * kvd
    o_b = o_h + 3 * N_HEADS
    o_cg, o_x, o_m = o_b + D, o_b + 2 * D, o_b + 3 * D
    sl = lambda a, n: w_in[:, :, a:a + n]
    hg_pad = jnp.pad(sl(o_h, 3 * N_HEADS), ((0, 0), (0, 0), (0, LANES - 3 * N_HEADS)))
    w_cat = jnp.concatenate([sl(o_q, D), sl(o_b, D), sl(o_cg, D), sl(o_x, D), sl(o_m, D), sl(o_m + D, D),
                             sl(o_c, kvd), sl(o_s, kvd), sl(o_w, kvd), hg_pad], axis=2).astype(BF16)
    depth = w_in.shape[0]
    half = CMP_STRIDE * HEAD_DIM
    w1r = cmp_w1.reshape(depth, 2, 2, CMP_STRIDE, HEAD_DIM, CMP_HID)
    ab = jnp.concatenate([w1r[:, :, 0], w1r[:, :, 1]], axis=-1)
    z = jnp.zeros_like(ab)
    wp = jnp.concatenate([jnp.concatenate([ab, z], axis=-1), jnp.concatenate([z, ab], axis=-1)], axis=-2)
    z2 = jnp.zeros_like(cmp_w2)
    w2bd = jnp.concatenate([jnp.concatenate([cmp_w2, z2], axis=-1), jnp.concatenate([z2, cmp_w2], axis=-1)], axis=-2)
    pe8 = jnp.broadcast_to(cmp_pe.reshape(depth, 2, 1, 2 * half), (depth, 2, 8, 2 * half))
    return w_cat, wp.astype(BF16), cmp_w1.astype(BF16), w2bd.astype(BF16), pe8.astype(BF16)


def kernel(x_prompt, x_sample, cache_cmp_kv, cache_sel_kv, state_win_kv, state_conv, page_table, norm_g, w_in,
           cmp_pe, cmp_w1, cmp_w2, conv_w, w_up_attn, w_up_conv, w_o, w_ff1, w_ff2):
    B, T, D = x_prompt.shape
    NB, TS, _ = x_sample.shape
    depth = w_in.shape[0]
    n_pool, page = cache_cmp_kv.shape[1:3]
    wb = state_win_kv.shape[2]
    past_len = page_table.shape[1] * page
    tq_s = 8

    w_cat, wp, w1b, w2bd, pe8 = _prep_weights(w_in, cmp_pe, cmp_w1, cmp_w2)
    wa, wc, wo, wf1, wf2 = (w.astype(BF16) for w in (w_up_attn, w_up_conv, w_o, w_ff1, w_ff2))
    cache_c = cache_cmp_kv.reshape(depth, n_pool, page, KV_ROW)
    cache_s = cache_sel_kv.reshape(depth, n_pool, page, KV_ROW).astype(BF16)
    win_state = state_win_kv.reshape(depth, NB, wb, KV_ROW).astype(BF16)
    zs = lambda n: jnp.zeros((depth, NB, n, D), F32)
    p1 = jnp.concatenate([state_conv[:, :, 1:2], zs(TS - 1)], axis=2).reshape(depth, NB * TS, D)
    p2 = jnp.concatenate([state_conv, zs(TS - 2)], axis=2).reshape(depth, NB * TS, D)

    n_chunks_p = T // CMP_STRIDE
    n_cmp_p = n_chunks_p - CMP_BLOCK // CMP_STRIDE + 1
    n_sel_p = -(-T // SEL_BLOCK)
    smap_pt = _sel_map(n_chunks_p, n_cmp_p, n_sel_p).T[:-(-n_sel_p // 8) * 8]
    e_blk = jnp.asarray(np.arange(T)[:, None] // SEL_BLOCK == np.arange(LANES)[None, :], dtype=BF16)
    q_scale = HEAD_DIM ** -0.5 * LOG2E
    L = past_len + TS
    n_chunks_s = L // CMP_STRIDE
    smap_s = _sel_map(n_chunks_s, n_chunks_s - CMP_BLOCK // CMP_STRIDE + 1, -(-L // SEL_BLOCK))
    e_blk_s = jnp.asarray(np.arange(LANES)[:, None] == np.arange(past_len + LANES)[None, :] // SEL_BLOCK, dtype=BF16)

    tm_p = min(256, T)
    tq_p = min(128, T)
    kc = 256
    pad_t = lambda a, n: jnp.pad(a.reshape(NB, TS, a.shape[-1]), ((0, 0), (0, n - TS), (0, 0)))

    xp = x_prompt.reshape(B * T, D)
    xs = x_sample.reshape(NB * TS, D)
    outs = [[] for _ in range(8)]
    for l in range(depth):
        g = norm_g[l]
        (q, yc, ga, gc, kvc, kvs, kvw, ks_b, vs_t, kw_b, vw_t, hg, ust) = _proj(
            xp, g[0:1], w_cat[l], conv_w[l], tm=tm_p, seq_len=T, q_scale=q_scale)
        kcmp, vcmp_t = _compress(kvc.reshape(B, T, KV_ROW), wp[l], w1b[l], w2bd[l], pe8[l])
        o = _attn_prompt(q, hg, kcmp, vcmp_t, ks_b.reshape(B, T, KV_DIM), vs_t, kw_b.reshape(B, T, KV_DIM), vw_t,
                         e_blk, smap_pt, B=B, T=T, tq=tq_p, kc=min(kc, T))
        xp = _post(o, yc, ga, gc, xp, g, wa[l], wc[l], wo[l], wf1[l], wf2[l], tm=tm_p)
        kv5 = lambda a, n: a.reshape(n, -1, 2, N_KV, HEAD_DIM)
        outs[0].append(kv5(kvc, B))
        outs[2].append(kv5(kvs, B))
        outs[4].append(kv5(kvw, B)[:, -min(WINDOW, T):])
        outs[6].append(ust[:, 8 - (CONV_W - 1):])
        (q, yc, ga, gc, kvc, kvs, kvw, kvs_b, kvw_b, hg, u) = _proj(
            xs, g[0:1], w_cat[l], conv_w[l], tm=NB * TS, seq_len=TS, q_scale=q_scale, prefix=(p1[l], p2[l]))
        o = _attn_sample(page_table, pad_t(q, tq_s), pad_t(hg, tq_s), pad_t(kvs_b, 16), pad_t(kvw_b, 16), win_state,
                         cache_c, cache_s, wp[l], w1b[l], w2bd[l], pe8[l], smap_s, e_blk_s, layer=l, t_new=TS)
        o = o[:, :TS].reshape(NB * TS, D)
        xs = _post(o, yc, ga, gc, xs, g, wa[l], wc[l], wo[l], wf1[l], wf2[l], tm=min(256, NB * TS))
        outs[1].append(kv5(kvc, NB))
        outs[3].append(kv5(kvs, NB))
        outs[5].append(kv5(kvw, NB))
        outs[7].append(u.reshape(NB, TS, D)[:, TS - (CONV_W - 1):])
    st = [jnp.stack(o) for o in outs]
    st[5] = jnp.concatenate([state_win_kv, st[5]], axis=2)[:, :, -wb:]
    return (xp.reshape(B, T, D), xs.reshape(NB, TS, D), st[0], st[1], st[2], st[3], st[4], st[5], st[6], st[7])
```

```python
import functools

import numpy as np
import jax
import jax.numpy as jnp
from jax import lax
from jax.experimental import pallas as pl
from jax.experimental.pallas import tpu as pltpu

F32 = jnp.float32
BF16 = jnp.bfloat16

HEAD_DIM = 64
N_KV = 4
GROUP = 4
N_HEADS = N_KV * GROUP
KV_DIM = N_KV * HEAD_DIM
KV_ROW = 2 * KV_DIM
PAIR = 2 * HEAD_DIM
CMP_BLOCK = 32
CMP_STRIDE = 16
CMP_HID = 2 * HEAD_DIM
SEL_BLOCK = 64
TOP_N = 16
WINDOW = 512
CONV_W = 3
EPS = 1e-6
NEG = -1e30
BIG = 1e30
LOG2E = 1.4426950408889634
FORCE_BONUS = 1e4
LANES = 128
VMEM_LIMIT = 56 * 1024 * 1024


def _rms(x, g):
    return x * lax.rsqrt(jnp.mean(x * x, axis=-1, keepdims=True) + EPS) * g


def _sigmoid(x):
    return 1.0 / (1.0 + jnp.exp(-x))


def _dot(a, b):
    return jnp.dot(a, b, preferred_element_type=F32)


def _dot_nt(a, b):
    return lax.dot_general(a, b, (((1,), (1,)), ((), ())), preferred_element_type=F32)


def _const_spec(shape):
    nd = len(shape)
    return pl.BlockSpec(shape, lambda *_: (0,) * nd, pipeline_mode=pl.Buffered(1))


_D = 1024
_OFF_Q, _OFF_B, _OFF_C, _OFF_X, _OFF_GA, _OFF_GC = 0, _D, 2 * _D, 3 * _D, 4 * _D, 5 * _D
_OFF_KVC = 6 * _D
_OFF_KVS = _OFF_KVC + KV_ROW
_OFF_KVW = _OFF_KVS + KV_ROW
_OFF_HG = _OFF_KVW + KV_ROW
_NP = _OFF_HG + LANES


def _proj_kernel(*refs, carry_mode, tiles_per_seq, seq_len, q_scale):
    if carry_mode:
        (x_ref, g_ref, w_ref, cw_ref,
         q_ref, yc_ref, ga_ref, gc_ref, kvc_ref, kvs_ref, kvw_ref, ksb_ref, vst_ref, kwb_ref, vwt_ref, hg_ref, ust_ref,
         carry_ref) = refs
    else:
        (x_ref, g_ref, w_ref, cw_ref, p1_ref, p2_ref,
         q_ref, yc_ref, ga_ref, gc_ref, kvc_ref, kvs_ref, kvw_ref, kvsb_ref, kvwb_ref, hg_ref, ust_ref) = refs
    tm = x_ref.shape[0]
    h = _rms(x_ref[...], g_ref[...]).astype(BF16)

    def mm(off, width):
        return _dot(h, w_ref[:, off:off + width])

    q_ref[...] = (mm(_OFF_Q, _D) * q_scale).astype(BF16)
    u = mm(_OFF_C, _D) * mm(_OFF_X, _D)
    row = lax.broadcasted_iota(jnp.int32, (tm, 1), 0)
    r1 = pltpu.roll(u, 1, axis=0)
    r2 = pltpu.roll(u, 2, axis=0)
    if carry_mode:
        @pl.when(pl.program_id(0) % tiles_per_seq == 0)
        def _():
            carry_ref[...] = jnp.zeros_like(carry_ref)
        c6 = carry_ref[6:7, :]
        c7 = carry_ref[7:8, :]
        u_m1 = jnp.where(row == 0, c7, r1)
        u_m2 = jnp.where(row == 0, c6, jnp.where(row == 1, c7, r2))
        tail = u[tm - 8:tm, :]
        carry_ref[...] = tail
        ust_ref[0] = tail
    else:
        t_in = row & (seq_len - 1)
        u_m1 = jnp.where(t_in == 0, p1_ref[...], r1)
        u_m2 = jnp.where(t_in < 2, p2_ref[...], r2)
        ust_ref[...] = u
    cw = cw_ref[...]
    yc_ref[...] = mm(_OFF_B, _D) * (cw[0:1] * u_m2 + cw[1:2] * u_m1 + cw[2:3] * u)
    ga_ref[...] = _sigmoid(mm(_OFF_GA, _D))
    gc_ref[...] = _sigmoid(mm(_OFF_GC, _D))
    kvc_ref[...] = mm(_OFF_KVC, KV_ROW)
    kvs = mm(_OFF_KVS, KV_ROW)
    kvs_ref[...] = kvs
    kvw = mm(_OFF_KVW, KV_ROW)
    kvw_ref[...] = kvw
    if carry_mode:
        ksb_ref[...] = kvs[:, :KV_DIM].astype(BF16)
        vst_ref[0] = kvs[:, KV_DIM:].T.astype(BF16)
        kwb_ref[...] = kvw[:, :KV_DIM].astype(BF16)
        vwt_ref[0] = kvw[:, KV_DIM:].T.astype(BF16)
    else:
        kvsb_ref[...] = kvs.astype(BF16)
        kvwb_ref[...] = kvw.astype(BF16)
    hg_ref[...] = _sigmoid(mm(_OFF_HG, LANES))


def _proj(x, g, w, cw, *, tm, seq_len, q_scale, prefix=None):
    M, D = x.shape
    carry_mode = prefix is None
    n_tiles = M // tm
    row_spec = lambda width: pl.BlockSpec((tm, width), lambda i: (i, 0))
    in_specs = [row_spec(D), _const_spec((1, D)), _const_spec((D, _NP)), _const_spec((CONV_W, D))]
    args = [x, g, w, cw]
    scratch = []
    if carry_mode:
        assert seq_len % tm == 0
        tiles_per_seq = seq_len // tm
        n_seq = M // seq_len
        ust_shape = jax.ShapeDtypeStruct((n_seq, 8, D), F32)
        ust_spec = pl.BlockSpec((1, 8, D), lambda i: (i // tiles_per_seq, 0, 0))
        scratch = [pltpu.VMEM((8, D), F32)]
        vt_shape = jax.ShapeDtypeStruct((n_seq, KV_DIM, seq_len), BF16)
        vt_spec = pl.BlockSpec((1, KV_DIM, tm), lambda i: (i // tiles_per_seq, 0, i % tiles_per_seq))
        kv_copies = [(jax.ShapeDtypeStruct((M, KV_DIM), BF16), row_spec(KV_DIM)), (vt_shape, vt_spec)] * 2
    else:
        assert tm % seq_len == 0 and seq_len >= CONV_W - 1
        tiles_per_seq = 1
        in_specs += [row_spec(D), row_spec(D)]
        args += list(prefix)
        ust_shape = jax.ShapeDtypeStruct((M, D), F32)
        ust_spec = row_spec(D)
        kv_copies = [(jax.ShapeDtypeStruct((M, KV_ROW), BF16), row_spec(KV_ROW))] * 2
    outs = ([(jax.ShapeDtypeStruct((M, D), BF16), row_spec(D))]
            + [(jax.ShapeDtypeStruct((M, D), F32), row_spec(D))] * 3
            + [(jax.ShapeDtypeStruct((M, KV_ROW), F32), row_spec(KV_ROW))] * 3
            + kv_copies
            + [(jax.ShapeDtypeStruct((M, LANES), F32), row_spec(LANES)),
               (ust_shape, ust_spec)])
    out_shape = [o[0] for o in outs]
    out_specs = [o[1] for o in outs]
    return pl.pallas_call(
        functools.partial(_proj_kernel, carry_mode=carry_mode, tiles_per_seq=tiles_per_seq, seq_len=seq_len,
                          q_scale=q_scale),
        grid=(n_tiles,), in_specs=in_specs, out_specs=out_specs, out_shape=out_shape,
        scratch_shapes=scratch, name="proj_carry" if carry_mode else "proj_prefix",
        compiler_params=pltpu.CompilerParams(dimension_semantics=("arbitrary",), vmem_limit_bytes=VMEM_LIMIT),
    )(*args)


def _pe_bias(pe_ref, w1_ref, kv):
    return _dot(pe_ref[kv], w1_ref[kv])[0:1, :]


def _compress_pair(load_rows, wp_ref, w2_ref, peb, kv, n_chunks):
    lhs = jnp.concatenate([load_rows(pos).astype(BF16) for pos in range(CMP_STRIDE)], axis=1)
    acc = _dot(lhs, wp_ref[kv])
    hid = []
    for gl in range(2):
        a = acc[:, gl * 2 * CMP_HID: gl * 2 * CMP_HID + CMP_HID]
        b = acc[:, gl * 2 * CMP_HID + CMP_HID: (gl + 1) * 2 * CMP_HID]
        hid.append(jnp.maximum(a + pltpu.roll(b, n_chunks - 1, axis=0) + peb, 0.0))
    return _dot(jnp.concatenate(hid, axis=1).astype(BF16), w2_ref[kv])


def _compress_kernel(k_ref, v_ref, wp_ref, w1_ref, w2_ref, pe_ref, kc_ref, vct_ref, *, n_chunks):
    for kv, src_ref in ((0, k_ref), (1, v_ref)):
        load = lambda pos, src_ref=src_ref: src_ref[0, pl.ds(pos, n_chunks, stride=CMP_STRIDE), :]
        peb = _pe_bias(pe_ref, w1_ref, kv)
        out = _compress_pair(load, wp_ref, w2_ref, peb, kv, n_chunks)
        if kv == 0:
            kc_ref[0] = out.astype(BF16)
        else:
            vct_ref[0] = out.T.astype(BF16)


def _compress(kvc, wp, w1, w2, pe):
    B, T, _ = kvc.shape
    n_chunks = T // CMP_STRIDE
    n_pairs = KV_DIM // PAIR
    return pl.pallas_call(
        functools.partial(_compress_kernel, n_chunks=n_chunks),
        grid=(B, n_pairs),
        in_specs=[pl.BlockSpec((1, T, PAIR), lambda b, j: (b, 0, j)),
                  pl.BlockSpec((1, T, PAIR), lambda b, j: (b, 0, n_pairs + j)),
                  _const_spec(wp.shape), _const_spec(w1.shape), _const_spec(w2.shape), _const_spec(pe.shape)],
        out_specs=[pl.BlockSpec((1, n_chunks, PAIR), lambda b, j: (b, 0, j)),
                   pl.BlockSpec((1, PAIR, n_chunks), lambda b, j: (b, j, 0))],
        out_shape=[jax.ShapeDtypeStruct((B, n_chunks, KV_DIM), BF16),
                   jax.ShapeDtypeStruct((B, KV_DIM, n_chunks), BF16)],
        name="compress",
        compiler_params=pltpu.CompilerParams(dimension_semantics=("arbitrary", "arbitrary"),
                                             vmem_limit_bytes=VMEM_LIMIT),
    )(kvc, kvc, wp, w1, w2, pe)


def _build_q2(q, pr, tq):
    zero = jnp.zeros((tq, HEAD_DIM), F32)
    g0, g1 = [], []
    for r in range(GROUP):
        h0 = (2 * pr) * GROUP + r
        h1 = (2 * pr + 1) * GROUP + r
        g0.append(jnp.concatenate([q[:, h0 * HEAD_DIM:(h0 + 1) * HEAD_DIM].astype(F32), zero], axis=1))
        g1.append(jnp.concatenate([zero, q[:, h1 * HEAD_DIM:(h1 + 1) * HEAD_DIM].astype(F32)], axis=1))
    return jnp.concatenate(g0 + g1, axis=0).astype(BF16)


COL_BLOCK = 2 * LANES


class _Softmax:
    def __init__(self, m_ref, l_ref, acc_refs):
        self.m_ref, self.l_ref, self.acc_refs = m_ref, l_ref, acc_refs

    def reset(self):
        self.m_ref[...] = jnp.full(self.m_ref.shape, NEG, F32)
        self.l_ref[...] = jnp.zeros(self.l_ref.shape, F32)
        for acc_ref in self.acc_refs:
            acc_ref[...] = jnp.zeros(acc_ref.shape, F32)

    def update(self, score_fns, vt_fns, mask=None):
        m_all, l_all = self.m_ref[...], self.l_ref[...]
        m_out, l_out = [], []
        scores = [f() for f in score_fns]
        for k, s in enumerate(scores):
            cols = slice(k * COL_BLOCK, (k + 1) * COL_BLOCK)
            if mask is not None:
                s = jnp.where(mask, s, NEG)
            m_old = m_all[:, cols]
            m_new = jnp.maximum(m_old, jnp.max(s, axis=0, keepdims=True))
            alpha = jnp.exp2(m_old - m_new)
            p = jnp.exp2(s - m_new)
            l_out.append(alpha * l_all[:, cols] + jnp.sum(p, axis=0, keepdims=True))
            m_out.append(m_new)
            acc_ref = self.acc_refs[k]
            acc_ref[...] = alpha * acc_ref[...] + _dot(vt_fns[k](), p.astype(BF16))
        self.m_ref[...] = jnp.concatenate(m_out, axis=1)
        self.l_ref[...] = jnp.concatenate(l_out, axis=1)

    def result(self):
        acc = jnp.concatenate([acc_ref[...] for acc_ref in self.acc_refs], axis=1)
        return acc * (1.0 / jnp.maximum(self.l_ref[...], 1e-30))


def _attn_prompt_kernel(q_ref, hg_ref, kc_ref, vct_ref, ks_ref, vst_ref, kw_ref, vwt_ref, e_ref, smapt_ref,
                        o_ref, m_ref, l_ref, *acc_refs, tq, kc, n_cmp, n_sel):
    i = pl.program_id(1)
    q0 = i * tq
    n_pairs = KV_DIM // PAIR
    M = 2 * GROUP * tq
    blocks_per_pair = M // COL_BLOCK
    col = lax.broadcasted_iota(jnp.int32, (1, M), 1)
    qpos = q0 + (col & (tq - 1))
    qpos_t = qpos[:, 0:tq]
    qpos_blk = qpos[:, 0:COL_BLOCK]
    q = q_ref[...]
    hg_t = hg_ref[...].T
    n_chunks = kc_ref.shape[1]
    nsp = smapt_ref.shape[0]
    n_top = min(TOP_N, n_sel)
    n_row = lax.broadcasted_iota(jnp.int32, (n_chunks, 1), 0)
    j_row = lax.broadcasted_iota(jnp.int32, (nsp, 1), 0)
    sm = _Softmax(m_ref, l_ref, acc_refs)
    win_steps = WINDOW // tq
    pair_lanes = lambda pr: pl.ds(pr * PAIR, PAIR)
    group_rows = lambda pr, k: pl.ds(pr * PAIR + (k * 2 // blocks_per_pair) * HEAD_DIM, HEAD_DIM)
    blk_cols = lambda k: slice(k * COL_BLOCK, (k + 1) * COL_BLOCK)
    pair_blocks = [(pr, k) for pr in range(n_pairs) for k in range(blocks_per_pair)]

    q2s, rhs_sels, o_cmps = [], [], []
    for pr in range(n_pairs):
        q2 = _build_q2(q, pr, tq)
        q2s.append(q2)

        s = _dot_nt(kc_ref[0, :, pair_lanes(pr)], q2)
        cmask = (n_row * CMP_STRIDE + (CMP_BLOCK - 1) <= qpos) & (n_row < n_cmp)
        s = jnp.where(cmask, s, NEG)
        e = jnp.where(cmask, jnp.exp2(s - jnp.max(s, axis=0, keepdims=True)), 0.0)
        p_cmp = e * (1.0 / jnp.maximum(jnp.sum(e, axis=0, keepdims=True), 1e-30))
        o_cmps.append(jnp.concatenate([_dot(vct_ref[0, group_rows(pr, k), :], p_cmp[:, blk_cols(k)].astype(BF16))
                                       for k in range(blocks_per_pair)], axis=1))

        cur = qpos_t >> 6
        forced = (j_row == 0) | (j_row == cur) | (j_row == cur - 1)
        causal_blk = j_row * SEL_BLOCK <= qpos_t
        bias_rows = []
        for gl in range(2):
            psum = p_cmp[:, (gl * GROUP) * tq:(gl * GROUP + 1) * tq]
            for r in range(1, GROUP):
                psum = psum + p_cmp[:, (gl * GROUP + r) * tq:(gl * GROUP + r + 1) * tq]
            p_hi = psum.astype(BF16)
            p_lo = (psum - p_hi.astype(F32)).astype(BF16)
            imp = _dot(smapt_ref[...], p_hi) + _dot(smapt_ref[...], p_lo)
            imp = jnp.where(causal_blk, imp + jnp.where(forced, FORCE_BONUS, 0.0), NEG)
            imp = jnp.where(j_row < n_sel, imp, -jnp.inf)
            cnt = jnp.zeros((nsp, tq), F32)
            for jb in range(n_sel):
                ri = imp[jb:jb + 1, :]
                tie = jnp.where(j_row > jb, 1.0, 0.0)
                cnt = cnt + jnp.where(ri > imp, 1.0, jnp.where(ri == imp, tie, 0.0))
            bias_t = jnp.where(cnt < n_top, 0.0, -BIG)
            bias_t = jnp.concatenate([bias_t, jnp.zeros((LANES - nsp, tq), F32)], axis=0)
            bias_rows += [bias_t.T.astype(BF16)] * GROUP
        rhs_sels.append(jnp.concatenate([q2, jnp.concatenate(bias_rows, axis=0)], axis=1))

    def sel_update(c, mask=None):
        rows = pl.ds(pl.multiple_of(c * kc, kc), kc)
        one_hot = e_ref[rows, :]
        lhs = [jnp.concatenate([ks_ref[0, rows, pair_lanes(pr)], one_hot], axis=1) for pr in range(n_pairs)]
        sm.update([lambda pr=pr, k=k: _dot_nt(lhs[pr], rhs_sels[pr][blk_cols(k), :]) for pr, k in pair_blocks],
                  [lambda pr=pr, k=k: vst_ref[0, group_rows(pr, k), rows] for pr, k in pair_blocks], mask)

    sm.reset()
    c_diag = (q0 + tq - 1) // kc

    def sel_body(c, carry):
        sel_update(c)
        return carry
    lax.fori_loop(0, c_diag, sel_body, 0)
    sel_update(c_diag, c_diag * kc + lax.broadcasted_iota(jnp.int32, (kc, 1), 0) <= qpos_blk)
    o_sel = sm.result()

    def win_update(c, mask=None):
        rows = pl.ds(pl.multiple_of(c * tq, tq), tq)
        k2 = [kw_ref[0, rows, pair_lanes(pr)] for pr in range(n_pairs)]
        sm.update([lambda pr=pr, k=k: _dot_nt(k2[pr], q2s[pr][blk_cols(k), :]) for pr, k in pair_blocks],
                  [lambda pr=pr, k=k: vwt_ref[0, group_rows(pr, k), rows] for pr, k in pair_blocks], mask)

    sm.reset()
    k_row = lax.broadcasted_iota(jnp.int32, (tq, 1), 0)

    @pl.when(i >= win_steps)
    def _():
        win_update(i - win_steps, q0 - WINDOW + k_row > qpos_blk - WINDOW)

    def win_body(c, carry):
        win_update(c)
        return carry
    lax.fori_loop(jnp.maximum(i - win_steps + 1, 0), i, win_body, 0)
    win_update(i, q0 + k_row <= qpos_blk)
    o_win = sm.result()

    for pr in range(n_pairs):
        def gate_row(br):
            head0 = 2 * pr * GROUP
            return jnp.concatenate([hg_t[(head0 + j) * 3 + br:(head0 + j) * 3 + br + 1, :]
                                    for j in range(2 * GROUP)], axis=1)
        pair_cols = slice(pr * M, (pr + 1) * M)
        gated = gate_row(0) * o_cmps[pr] + gate_row(1) * o_sel[:, pair_cols] + gate_row(2) * o_win[:, pair_cols]
        for k in range(blocks_per_pair):
            c0 = k * COL_BLOCK
            two_heads = jnp.concatenate([gated[:, c0:c0 + tq], gated[:, c0 + tq:c0 + 2 * tq]], axis=0)
            o_blk = pr * blocks_per_pair + k
            o_ref[:, o_blk * LANES:(o_blk + 1) * LANES] = two_heads.T.astype(o_ref.dtype)


def _attn_prompt(q, hg, kcmp, vcmp_t, ks, vs_t, kw, vw_t, e_blk, smap_t, *, B, T, tq, kc):
    n_t = T // tq
    n_chunks = T // CMP_STRIDE
    n_cmp = n_chunks - CMP_BLOCK // CMP_STRIDE + 1
    n_sel = -(-T // SEL_BLOCK)
    assert T % kc == 0 and T % tq == 0 and kc % tq == 0 and WINDOW % tq == 0 and tq == LANES
    assert n_sel <= LANES and n_chunks <= LANES
    M = 2 * GROUP * tq
    D = q.shape[1]
    seq_rows = lambda width: pl.BlockSpec((1, T, width), lambda b, i: (b, 0, 0))
    seq_cols = pl.BlockSpec((1, KV_DIM, T), lambda b, i: (b, 0, 0))
    return pl.pallas_call(
        functools.partial(_attn_prompt_kernel, tq=tq, kc=kc, n_cmp=n_cmp, n_sel=n_sel),
        grid=(B, n_t),
        in_specs=[pl.BlockSpec((tq, D), lambda b, i: (b * n_t + i, 0)),
                  pl.BlockSpec((tq, LANES), lambda b, i: (b * n_t + i, 0)),
                  pl.BlockSpec((1, n_chunks, KV_DIM), lambda b, i: (b, 0, 0)),
                  pl.BlockSpec((1, KV_DIM, n_chunks), lambda b, i: (b, 0, 0)),
                  seq_rows(KV_DIM), seq_cols, seq_rows(KV_DIM), seq_cols,
                  _const_spec(e_blk.shape), _const_spec(smap_t.shape)],
        out_specs=pl.BlockSpec((tq, D), lambda b, i: (b * n_t + i, 0)),
        out_shape=jax.ShapeDtypeStruct((B * T, D), BF16),
        scratch_shapes=([pltpu.VMEM((1, 2 * M), F32), pltpu.VMEM((1, 2 * M), F32)]
                        + [pltpu.VMEM((HEAD_DIM, COL_BLOCK), F32)] * (2 * M // COL_BLOCK)),
        name="attn_prompt",
        compiler_params=pltpu.CompilerParams(dimension_semantics=("arbitrary", "arbitrary"),
                                             vmem_limit_bytes=VMEM_LIMIT),
    )(q, hg, kcmp, vcmp_t, ks, vs_t, kw, vw_t, e_blk, smap_t)


def _softmax_rows(parts):
    m = functools.reduce(jnp.maximum, [jnp.max(s, axis=1, keepdims=True) for s in parts])
    ps = [jnp.exp2(s - m) for s in parts]
    inv = 1.0 / jnp.maximum(functools.reduce(jnp.add, [jnp.sum(p, axis=1, keepdims=True) for p in ps]), 1e-30)
    return [p * inv for p in ps]


def _attn_sample_kernel(pt_ref, q_ref, hg_ref, kvs_new_ref, kvw_new_ref, win_ref, cache_c_ref, cache_s_ref,
                        wp_ref, w1_ref, w2_ref, pe_ref, smap_ref, e_ref,
                        o_ref,
                        buf_c, buf_s, peb_ref, sem_c, sem_s,
                        *, layer, n_pages, page, past_len, tq, n_cmp, n_sel, wb):
    b = pl.program_id(0)
    nb = pl.num_programs(0)
    slot = b % 2

    def page_copies(bb, sl):
        out = []
        chunks_per_page = page // CMP_STRIDE
        for p in range(n_pages):
            pg = pt_ref[bb, p]
            out.append(pltpu.make_async_copy(cache_c_ref.at[layer, pg],
                                             buf_c.at[sl, pl.ds(p * chunks_per_page, chunks_per_page)], sem_c.at[sl]))
            out.append(pltpu.make_async_copy(cache_s_ref.at[layer, pg], buf_s.at[sl, pl.ds(p * page, page)],
                                             sem_s.at[sl]))
        return out

    @pl.when(b == 0)
    def _():
        for cp in page_copies(0, 0):
            cp.start()
        for kv in range(2):
            peb_ref[kv] = jnp.broadcast_to(_pe_bias(pe_ref, w1_ref, kv), (8, CMP_HID))

    @pl.when(b + 1 < nb)
    def _():
        for cp in page_copies(b + 1, 1 - slot):
            cp.start()

    for cp in page_copies(b, slot):
        cp.wait()

    M = N_HEADS * tq
    n_new = LANES
    pad_new = lambda ref: jnp.concatenate([ref[0], jnp.zeros((n_new - ref.shape[1], KV_ROW), BF16)], axis=0)
    kvs_new = pad_new(kvs_new_ref)
    kvw_new = pad_new(kvw_new_ref)
    row = lax.broadcasted_iota(jnp.int32, (M, 1), 0)
    qpos = past_len + (row & (tq - 1))
    lane = lax.broadcasted_iota(jnp.int32, (1, LANES), 1)
    q = q_ref[0].astype(F32)
    hg = hg_ref[0]
    n_chunks = past_len // CMP_STRIDE

    blocks = []
    for h in range(N_HEADS):
        g = h // GROUP
        pieces = [jnp.zeros((tq, HEAD_DIM * g), F32)] if g else []
        pieces.append(q[:, h * HEAD_DIM:(h + 1) * HEAD_DIM])
        if g < N_KV - 1:
            pieces.append(jnp.zeros((tq, HEAD_DIM * (N_KV - 1 - g)), F32))
        blocks.append(jnp.concatenate(pieces, axis=1) if len(pieces) > 1 else pieces[0])
    q4 = jnp.concatenate(blocks, axis=0).astype(BF16)

    cmp_kv = []
    for kv in range(2):
        halves = []
        for pr in range(KV_DIM // PAIR):
            lane0 = kv * KV_DIM + pr * PAIR
            load = lambda pos, lane0=lane0: buf_c[slot, :, pl.ds(pos * KV_ROW + lane0, PAIR)]
            halves.append(_compress_pair(load, wp_ref, w2_ref, peb_ref[kv][0:1, :], kv, n_chunks))
        cmp_kv.append(jnp.concatenate(halves, axis=1).astype(BF16))

    s = _dot_nt(q4, cmp_kv[0])
    n_idx = lax.broadcasted_iota(jnp.int32, (1, n_chunks), 1)
    cmask = (n_idx * CMP_STRIDE + (CMP_BLOCK - 1) <= qpos) & (n_idx < n_cmp)
    s = jnp.where(cmask, s, NEG)
    e = jnp.where(cmask, jnp.exp2(s - jnp.max(s, axis=1, keepdims=True)), 0.0)
    p_cmp = e * (1.0 / jnp.maximum(jnp.sum(e, axis=1, keepdims=True), 1e-30))
    o_cmp = _dot(p_cmp.astype(BF16), cmp_kv[1])

    psum = []
    for g in range(N_KV):
        acc = p_cmp[(g * GROUP) * tq:(g * GROUP + 1) * tq, :]
        for r in range(1, GROUP):
            acc = acc + p_cmp[(g * GROUP + r) * tq:(g * GROUP + r + 1) * tq, :]
        psum.append(acc)
    psum = jnp.concatenate(psum, axis=0)
    p_hi = psum.astype(BF16)
    p_lo = (psum - p_hi.astype(F32)).astype(BF16)
    imp = _dot(p_hi, smap_ref[...]) + _dot(p_lo, smap_ref[...])
    qpos_g = qpos[0:N_KV * tq, :]
    cur = qpos_g >> 6
    forced = (lane == 0) | (lane == cur) | (lane == cur - 1)
    imp = jnp.where(lane * SEL_BLOCK <= qpos_g, imp + jnp.where(forced, FORCE_BONUS, 0.0), NEG)
    imp = jnp.where(lane < n_sel, imp, -jnp.inf)
    cnt = jnp.zeros(imp.shape, F32)
    for jb in range(n_sel):
        ci = imp[:, jb:jb + 1]
        tie = jnp.where(lane > jb, 1.0, 0.0)
        cnt = cnt + jnp.where(ci > imp, 1.0, jnp.where(ci == imp, tie, 0.0))
    neg = jnp.where(cnt < min(TOP_N, n_sel), 0.0, -BIG)
    neg = jnp.concatenate([neg[g * tq:(g + 1) * tq, :] for g in range(N_KV) for _ in range(GROUP)], axis=0)
    bias = _dot(neg.astype(BF16), e_ref[...])

    new_pos = past_len + lax.broadcasted_iota(jnp.int32, (1, n_new), 1)
    s_old = _dot_nt(q4, buf_s[slot, :, 0:KV_DIM].astype(BF16)) + bias[:, 0:past_len]
    s_new = _dot_nt(q4, kvs_new[:, 0:KV_DIM]) + bias[:, past_len:past_len + n_new]
    p_old, p_new = _softmax_rows([s_old, jnp.where(new_pos <= qpos, s_new, NEG)])
    o_sel = (_dot(p_old.astype(BF16), buf_s[slot, :, KV_DIM:KV_ROW].astype(BF16))
             + _dot(p_new.astype(BF16), kvs_new[:, KV_DIM:KV_ROW]))

    st_pos = past_len - wb + lax.broadcasted_iota(jnp.int32, (1, wb), 1)
    s_st = jnp.where(st_pos > qpos - WINDOW, _dot_nt(q4, win_ref[0, :, 0:KV_DIM].astype(BF16)), NEG)
    s_nw = jnp.where(new_pos <= qpos, _dot_nt(q4, kvw_new[:, 0:KV_DIM]), NEG)
    p_st, p_nw = _softmax_rows([s_st, s_nw])
    o_win = (_dot(p_st.astype(BF16), win_ref[0, :, KV_DIM:KV_ROW].astype(BF16))
             + _dot(p_nw.astype(BF16), kvw_new[:, KV_DIM:KV_ROW]))

    gate = lambda br: jnp.concatenate([hg[:, h * 3 + br:h * 3 + br + 1] for h in range(N_HEADS)], axis=0)
    gated = gate(0) * o_cmp + gate(1) * o_sel + gate(2) * o_win

    for blk in range(N_HEADS // 2):
        g = (2 * blk) // GROUP
        lanes = slice((g // 2) * PAIR, (g // 2 + 1) * PAIR)
        even = gated[(2 * blk) * tq:(2 * blk + 1) * tq, lanes]
        odd = gated[(2 * blk + 1) * tq:(2 * blk + 2) * tq, lanes]
        if g % 2 == 0:
            odd = pltpu.roll(odd, HEAD_DIM, axis=1)
        else:
            even = pltpu.roll(even, HEAD_DIM, axis=1)
        o_ref[0, :, blk * LANES:(blk + 1) * LANES] = jnp.where(lane < HEAD_DIM, even, odd).astype(o_ref.dtype)


def _attn_sample(page_table, q, hg, kvs_new, kvw_new, win_state, cache_c, cache_s, wp, w1, w2, pe, smap, e_blk,
                 *, layer, t_new):
    nb, n_pages = page_table.shape
    page = cache_s.shape[2]
    assert cache_c.shape[2:] == (page // CMP_STRIDE, CMP_STRIDE * KV_ROW)
    past_len = n_pages * page
    wb = win_state.shape[2]
    tq = q.shape[1]
    n_new = kvs_new.shape[1]
    D = q.shape[2]
    L = past_len + t_new
    n_chunks = L // CMP_STRIDE
    assert past_len % CMP_STRIDE == 0 and n_chunks * CMP_STRIDE == past_len and n_chunks <= LANES
    assert t_new <= tq == 8 and t_new <= n_new <= LANES and wb <= WINDOW and e_blk.shape == (LANES, past_len + LANES)
    n_cmp = n_chunks - CMP_BLOCK // CMP_STRIDE + 1
    n_sel = -(-L // SEL_BLOCK)
    assert n_sel <= LANES
    grid_spec = pltpu.PrefetchScalarGridSpec(
        num_scalar_prefetch=1, grid=(nb,),
        in_specs=[pl.BlockSpec((1, tq, D), lambda b, pt: (b, 0, 0)),
                  pl.BlockSpec((1, tq, LANES), lambda b, pt: (b, 0, 0)),
                  pl.BlockSpec((1, n_new, KV_ROW), lambda b, pt: (b, 0, 0)),
                  pl.BlockSpec((1, n_new, KV_ROW), lambda b, pt: (b, 0, 0)),
                  pl.BlockSpec((None, 1, wb, KV_ROW), lambda b, pt: (layer, b, 0, 0)),
                  pl.BlockSpec(memory_space=pl.ANY),
                  pl.BlockSpec(memory_space=pl.ANY),
                  _const_spec(wp.shape), _const_spec(w1.shape), _const_spec(w2.shape), _const_spec(pe.shape),
                  _const_spec(smap.shape), _const_spec(e_blk.shape)],
        out_specs=pl.BlockSpec((1, tq, D), lambda b, pt: (b, 0, 0)),
        scratch_shapes=[pltpu.VMEM((2, n_chunks, CMP_STRIDE * KV_ROW), F32),
                        pltpu.VMEM((2, past_len, KV_ROW), F32),
                        pltpu.VMEM((2, 8, CMP_HID), F32),
                        pltpu.SemaphoreType.DMA((2,)), pltpu.SemaphoreType.DMA((2,))])
    return pl.pallas_call(
        functools.partial(_attn_sample_kernel, layer=layer, n_pages=n_pages, page=page, past_len=past_len,
                          tq=tq, n_cmp=n_cmp, n_sel=n_sel, wb=wb),
        grid_spec=grid_spec,
        out_shape=jax.ShapeDtypeStruct((nb, tq, D), BF16),
        name="attn_sample",
        compiler_params=pltpu.CompilerParams(dimension_semantics=("arbitrary",), vmem_limit_bytes=VMEM_LIMIT),
    )(page_table, q, hg, kvs_new, kvw_new, win_state, cache_c, cache_s, wp, w1, w2, pe, smap, e_blk)


def _post_kernel(o_ref, yc_ref, ga_ref, gc_ref, x_ref, g_ref, wa_ref, wc_ref, wo_ref, w1_ref, w2_ref, out_ref,
                 *, ff_chunk):
    g = g_ref[...]
    a = _dot(o_ref[...], wa_ref[...])
    c = _dot(yc_ref[...].astype(BF16), wc_ref[...])
    mix = _dot((ga_ref[...] * a + gc_ref[...] * c).astype(BF16), wo_ref[...])
    x1 = x_ref[...] + _rms(mix, g[1:2])
    h2 = _rms(x1, g[2:3]).astype(BF16)
    f = jnp.zeros(x1.shape, F32)
    for j in range(w1_ref.shape[1] // ff_chunk):
        t = jnp.maximum(_dot(h2, w1_ref[:, j * ff_chunk:(j + 1) * ff_chunk]), 0.0)
        f = f + _dot((t * t).astype(BF16), w2_ref[j * ff_chunk:(j + 1) * ff_chunk, :])
    out_ref[...] = x1 + _rms(f, g[3:4])


def _post(o, yc, ga, gc, x, g, wa, wc, wo, w1, w2, *, tm):
    M, D = x.shape
    row_spec = pl.BlockSpec((tm, D), lambda i: (i, 0))
    return pl.pallas_call(
        functools.partial(_post_kernel, ff_chunk=1024),
        grid=(M // tm,),
        in_specs=[row_spec] * 5 + [_const_spec(a.shape) for a in (g, wa, wc, wo, w1, w2)],
        out_specs=row_spec,
        out_shape=jax.ShapeDtypeStruct((M, D), F32),
        name="post",
        compiler_params=pltpu.CompilerParams(dimension_semantics=("arbitrary",), vmem_limit_bytes=VMEM_LIMIT),
    )(o, yc, ga, gc, x, g, wa, wc, wo, w1, w2)


def _sel_map(n_chunks, n_cmp, n_sel):
    cs = np.arange(n_chunks)[:, None] * CMP_STRIDE
    ss = np.arange(LANES)[None, :] * SEL_BLOCK
    ov = np.clip(np.minimum(cs + CMP_BLOCK, ss + SEL_BLOCK) - np.maximum(cs, ss), 0, None) / CMP_STRIDE
    ov = ov * (np.arange(n_chunks)[:, None] < n_cmp) * (np.arange(LANES)[None, :] < n_sel)
    return jnp.asarray(ov, dtype=BF16)


def _prep_weights(w_in, cmp_pe, cmp_w1, cmp_w2):
    D = w_in.shape[1]
    assert D == _D
    kvd = 2 * KV_DIM
    o_q, o_c, o_s, o_w, o_h = 0, D, D + kvd, D + 2 * kvd, D + 3 * kvd
    o_b = o_h + 3 * N_HEADS
    o_cg, o_x, o_m = o_b + D, o_b + 2 * D, o_b + 3 * D
    sl = lambda a, n: w_in[:, :, a:a + n]
    hg_pad = jnp.pad(sl(o_h, 3 * N_HEADS), ((0, 0), (0, 0), (0, LANES - 3 * N_HEADS)))
    w_cat = jnp.concatenate([sl(o_q, D), sl(o_b, D), sl(o_cg, D), sl(o_x, D), sl(o_m, D), sl(o_m + D, D),
                             sl(o_c, kvd), sl(o_s, kvd), sl(o_w, kvd), hg_pad], axis=2).astype(BF16)
    depth = w_in.shape[0]
    half = CMP_STRIDE * HEAD_DIM
    w1r = cmp_w1.reshape(depth, 2, 2, CMP_STRIDE, HEAD_DIM, CMP_HID)
    ab = jnp.concatenate([w1r[:, :, 0], w1r[:, :, 1]], axis=-1)
    z = jnp.zeros_like(ab)
    wp = jnp.concatenate([jnp.concatenate([ab, z], axis=-1), jnp.concatenate([z, ab], axis=-1)], axis=-2)
    wp = wp.reshape(depth, 2, CMP_STRIDE * PAIR, 4 * CMP_HID)
    z2 = jnp.zeros_like(cmp_w2)
    w2bd = jnp.concatenate([jnp.concatenate([cmp_w2, z2], axis=-1), jnp.concatenate([z2, cmp_w2], axis=-1)], axis=-2)
    pe8 = jnp.broadcast_to(cmp_pe.reshape(depth, 2, 1, 2 * half), (depth, 2, 8, 2 * half))
    return w_cat, wp.astype(BF16), cmp_w1.astype(BF16), w2bd.astype(BF16), pe8.astype(BF16)


def kernel(x_prompt, x_sample, cache_cmp_kv, cache_sel_kv, state_win_kv, state_conv, page_table, norm_g, w_in,
           cmp_pe, cmp_w1, cmp_w2, conv_w, w_up_attn, w_up_conv, w_o, w_ff1, w_ff2):
    B, T, D = x_prompt.shape
    NB, TS, _ = x_sample.shape
    depth = w_in.shape[0]
    n_pool, page = cache_cmp_kv.shape[1:3]
    wb = state_win_kv.shape[2]
    past_len = page_table.shape[1] * page
    tq_s = 8

    w_cat, wp, w1b, w2bd, pe8 = _prep_weights(w_in, cmp_pe, cmp_w1, cmp_w2)
    wa, wc, wo, wf1, wf2 = (w.astype(BF16) for w in (w_up_attn, w_up_conv, w_o, w_ff1, w_ff2))
    cache_c = cache_cmp_kv.reshape(depth, n_pool, page // CMP_STRIDE, CMP_STRIDE * KV_ROW)
    cache_s = cache_sel_kv.reshape(depth, n_pool, page, KV_ROW)
    win_state = state_win_kv.reshape(depth, NB, wb, KV_ROW)
    zs = lambda n: jnp.zeros((depth, NB, n, D), F32)
    p1 = jnp.concatenate([state_conv[:, :, 1:2], zs(TS - 1)], axis=2).reshape(depth, NB * TS, D)
    p2 = jnp.concatenate([state_conv, zs(TS - 2)], axis=2).reshape(depth, NB * TS, D)

    n_chunks_p = T // CMP_STRIDE
    n_cmp_p = n_chunks_p - CMP_BLOCK // CMP_STRIDE + 1
    n_sel_p = -(-T // SEL_BLOCK)
    smap_pt = _sel_map(n_chunks_p, n_cmp_p, n_sel_p).T[:-(-n_sel_p // 8) * 8]
    e_blk = jnp.asarray(np.arange(T)[:, None] // SEL_BLOCK == np.arange(LANES)[None, :], dtype=BF16)
    q_scale = HEAD_DIM ** -0.5 * LOG2E
    L = past_len + TS
    n_chunks_s = L // CMP_STRIDE
    smap_s = _sel_map(n_chunks_s, n_chunks_s - CMP_BLOCK // CMP_STRIDE + 1, -(-L // SEL_BLOCK))
    e_blk_s = jnp.asarray(np.arange(LANES)[:, None] == np.arange(past_len + LANES)[None, :] // SEL_BLOCK, dtype=BF16)

    tm_p = min(256, T)
    tq_p = min(128, T)
    kc = 256
    pad_t = lambda a, n: jnp.pad(a.reshape(NB, TS, a.shape[-1]), ((0, 0), (0, n - TS), (0, 0)))

    xp = x_prompt.reshape(B * T, D)
    xs = x_sample.reshape(NB * TS, D)
    outs = [[] for _ in range(8)]
    for l in range(depth):
        g = norm_g[l]
        (q, yc, ga, gc, kvc, kvs, kvw, ks_b, vs_t, kw_b, vw_t, hg, ust) = _proj(
            xp, g[0:1], w_cat[l], conv_w[l], tm=tm_p, seq_len=T, q_scale=q_scale)
        kcmp, vcmp_t = _compress(kvc.reshape(B, T, KV_ROW), wp[l], w1b[l], w2bd[l], pe8[l])
        o = _attn_prompt(q, hg, kcmp, vcmp_t, ks_b.reshape(B, T, KV_DIM), vs_t, kw_b.reshape(B, T, KV_DIM), vw_t,
                         e_blk, smap_pt, B=B, T=T, tq=tq_p, kc=min(kc, T))
        xp = _post(o, yc, ga, gc, xp, g, wa[l], wc[l], wo[l], wf1[l], wf2[l], tm=tm_p)
        kv5 = lambda a, n: a.reshape(n, -1, 2, N_KV, HEAD_DIM)
        outs[0].append(kv5(kvc, B))
        outs[2].append(kv5(kvs, B))
        outs[4].append(kv5(kvw, B)[:, -min(WINDOW, T):])
        outs[6].append(ust[:, 8 - (CONV_W - 1):])
        (q, yc, ga, gc, kvc, kvs, kvw, kvs_b, kvw_b, hg, u) = _proj(
            xs, g[0:1], w_cat[l], conv_w[l], tm=NB * TS, seq_len=TS, q_scale=q_scale, prefix=(p1[l], p2[l]))
        o = _attn_sample(page_table, pad_t(q, tq_s), pad_t(hg, tq_s), pad_t(kvs_b, 16), pad_t(kvw_b, 16), win_state,
                         cache_c, cache_s, wp[l], w1b[l], w2bd[l], pe8[l], smap_s, e_blk_s, layer=l, t_new=TS)
        o = o[:, :TS].reshape(NB * TS, D)
        xs = _post(o, yc, ga, gc, xs, g, wa[l], wc[l], wo[l], wf1[l], wf2[l], tm=min(256, NB * TS))
        outs[1].append(kv5(kvc, NB))
        outs[3].append(kv5(kvs, NB))
        outs[5].append(kv5(kvw, NB))
        outs[7].append(u.reshape(NB, TS, D)[:, TS - (CONV_W - 1):])
    st = [jnp.stack(o) for o in outs]
    st[5] = jnp.concatenate([state_win_kv, st[5]], axis=2)[:, :, -wb:]
    return (xp.reshape(B, T, D), xs.reshape(NB, TS, D), st[0], st[1], st[2], st[3], st[4], st[5], st[6], st[7])
```

```python
import functools

import numpy as np
import jax
import jax.numpy as jnp
from jax import lax
from jax.experimental import pallas as pl
from jax.experimental.pallas import tpu as pltpu

F32 = jnp.float32
BF16 = jnp.bfloat16

HEAD_DIM = 64
N_KV = 4
GROUP = 4
N_HEADS = N_KV * GROUP
KV_DIM = N_KV * HEAD_DIM
KV_ROW = 2 * KV_DIM
PAIR = 2 * HEAD_DIM
CMP_BLOCK = 32
CMP_STRIDE = 16
CMP_HID = 2 * HEAD_DIM
SEL_BLOCK = 64
TOP_N = 16
WINDOW = 512
CONV_W = 3
EPS = 1e-6
NEG = -1e30
BIG = 1e30
LOG2E = 1.4426950408889634
FORCE_BONUS = 1e4
LANES = 128
VMEM_LIMIT = 56 * 1024 * 1024


def _rms(x, g):
    return x * lax.rsqrt(jnp.mean(x * x, axis=-1, keepdims=True) + EPS) * g


def _sigmoid(x):
    return 1.0 / (1.0 + jnp.exp(-x))


def _dot(a, b):
    return jnp.dot(a, b, preferred_element_type=F32)


def _dot_nt(a, b):
    return lax.dot_general(a, b, (((1,), (1,)), ((), ())), preferred_element_type=F32)


def _const_spec(shape):
    nd = len(shape)
    return pl.BlockSpec(shape, lambda *_: (0,) * nd, pipeline_mode=pl.Buffered(1))


_D = 1024
_OFF_Q, _OFF_B, _OFF_C, _OFF_X, _OFF_GA, _OFF_GC = 0, _D, 2 * _D, 3 * _D, 4 * _D, 5 * _D
_OFF_KVC = 6 * _D
_OFF_KVS = _OFF_KVC + KV_ROW
_OFF_KVW = _OFF_KVS + KV_ROW
_OFF_HG = _OFF_KVW + KV_ROW
_NP = _OFF_HG + LANES


def _proj_kernel(*refs, carry_mode, tiles_per_seq, seq_len, q_scale):
    if carry_mode:
        (x_ref, g_ref, w_ref, cw_ref, _, _, _,
         q_ref, yc_ref, ga_ref, gc_ref, kvc_ref, kvs_ref, kvw_ref, ksb_ref, vst_ref, kwb_ref, vwt_ref, hg_ref, ust_ref,
         carry_ref) = refs
    else:
        (x_ref, g_ref, w_ref, cw_ref, p1_ref, p2_ref,
         q_ref, yc_ref, ga_ref, gc_ref, kvc_ref, kvs_ref, kvw_ref, kvsb_ref, kvwb_ref, hg_ref, ust_ref) = refs
    tm = x_ref.shape[0]
    h = _rms(x_ref[...], g_ref[...]).astype(BF16)

    def mm(off, width):
        return _dot(h, w_ref[:, off:off + width])

    q_ref[...] = (mm(_OFF_Q, _D) * q_scale).astype(BF16)
    u = mm(_OFF_C, _D) * mm(_OFF_X, _D)
    row = lax.broadcasted_iota(jnp.int32, (tm, 1), 0)
    r1 = pltpu.roll(u, 1, axis=0)
    r2 = pltpu.roll(u, 2, axis=0)
    if carry_mode:
        @pl.when(pl.program_id(0) % tiles_per_seq == 0)
        def _():
            carry_ref[...] = jnp.zeros_like(carry_ref)
        c6 = carry_ref[6:7, :]
        c7 = carry_ref[7:8, :]
        u_m1 = jnp.where(row == 0, c7, r1)
        u_m2 = jnp.where(row == 0, c6, jnp.where(row == 1, c7, r2))
        tail = u[tm - 8:tm, :]
        carry_ref[...] = tail
        ust_ref[0] = tail
    else:
        t_in = row & (seq_len - 1)
        u_m1 = jnp.where(t_in == 0, p1_ref[...], r1)
        u_m2 = jnp.where(t_in < 2, p2_ref[...], r2)
        ust_ref[...] = u
    cw = cw_ref[...]
    yc_ref[...] = mm(_OFF_B, _D) * (cw[0:1] * u_m2 + cw[1:2] * u_m1 + cw[2:3] * u)
    ga_ref[...] = _sigmoid(mm(_OFF_GA, _D))
    gc_ref[...] = _sigmoid(mm(_OFF_GC, _D))
    kvc_ref[...] = mm(_OFF_KVC, KV_ROW)
    kvs = mm(_OFF_KVS, KV_ROW)
    kvs_ref[...] = kvs
    kvw = mm(_OFF_KVW, KV_ROW)
    kvw_ref[...] = kvw
    if carry_mode:
        ksb_ref[...] = kvs[:, :KV_DIM].astype(BF16)
        vst_ref[0] = kvs[:, KV_DIM:].T.astype(BF16)
        kwb_ref[...] = kvw[:, :KV_DIM].astype(BF16)
        vwt_ref[0] = kvw[:, KV_DIM:].T.astype(BF16)
    else:
        kvsb_ref[...] = kvs.astype(BF16)
        kvwb_ref[...] = kvw.astype(BF16)
    hg_ref[...] = _sigmoid(mm(_OFF_HG, LANES))


def _proj(x, g, w, cw, *, tm, seq_len, q_scale, prefix=None, stacks=None, layer=0):
    M, D = x.shape
    carry_mode = prefix is None
    n_tiles = M // tm
    row_spec = lambda width: pl.BlockSpec((tm, width), lambda i: (i, 0))
    in_specs = [row_spec(D), _const_spec((1, D)), _const_spec((D, _NP)), _const_spec((CONV_W, D))]
    args = [x, g, w, cw]
    scratch = []
    if carry_mode:
        assert seq_len % tm == 0
        tiles_per_seq = seq_len // tm
        n_seq = M // seq_len
        ust_shape = jax.ShapeDtypeStruct((n_seq, 8, D), F32)
        ust_spec = pl.BlockSpec((1, 8, D), lambda i: (i // tiles_per_seq, 0, 0))
        scratch = [pltpu.VMEM((8, D), F32)]
        vt_shape = jax.ShapeDtypeStruct((n_seq, KV_DIM, seq_len), BF16)
        vt_spec = pl.BlockSpec((1, KV_DIM, tm), lambda i: (i // tiles_per_seq, 0, i % tiles_per_seq))
        kv_copies = [(jax.ShapeDtypeStruct((M, KV_DIM), BF16), row_spec(KV_DIM)), (vt_shape, vt_spec)] * 2
        in_specs += [pl.BlockSpec(memory_space=pl.ANY)] * 3
        args += list(stacks)
        aliases = {4: 4, 5: 5, 6: 6}
        kv_out = (jax.ShapeDtypeStruct(stacks[0].shape, F32),
                  pl.BlockSpec((tm, KV_ROW), lambda i: (layer * n_tiles + i, 0)))
    else:
        assert tm % seq_len == 0 and seq_len >= CONV_W - 1
        tiles_per_seq = 1
        in_specs += [row_spec(D), row_spec(D)]
        args += list(prefix)
        ust_shape = jax.ShapeDtypeStruct((M, D), F32)
        ust_spec = row_spec(D)
        kv_copies = [(jax.ShapeDtypeStruct((M, KV_ROW), BF16), row_spec(KV_ROW))] * 2
        aliases = {}
        kv_out = (jax.ShapeDtypeStruct((M, KV_ROW), F32), row_spec(KV_ROW))
    outs = ([(jax.ShapeDtypeStruct((M, D), BF16), row_spec(D))]
            + [(jax.ShapeDtypeStruct((M, D), F32), row_spec(D))] * 3
            + [kv_out] * 3
            + kv_copies
            + [(jax.ShapeDtypeStruct((M, LANES), F32), row_spec(LANES)),
               (ust_shape, ust_spec)])
    out_shape = [o[0] for o in outs]
    out_specs = [o[1] for o in outs]
    return pl.pallas_call(
        functools.partial(_proj_kernel, carry_mode=carry_mode, tiles_per_seq=tiles_per_seq, seq_len=seq_len,
                          q_scale=q_scale),
        grid=(n_tiles,), in_specs=in_specs, out_specs=out_specs, out_shape=out_shape,
        scratch_shapes=scratch, input_output_aliases=aliases, name="proj_carry" if carry_mode else "proj_prefix",
        compiler_params=pltpu.CompilerParams(dimension_semantics=("arbitrary",), vmem_limit_bytes=VMEM_LIMIT),
    )(*args)


def _pe_bias(pe_ref, w1_ref, kv):
    return _dot(pe_ref[kv], w1_ref[kv])[0:1, :]


def _compress_pair(load_rows, wp_ref, w2_ref, peb, kv, n_chunks):
    lhs = jnp.concatenate([load_rows(pos).astype(BF16) for pos in range(CMP_STRIDE)], axis=1)
    acc = _dot(lhs, wp_ref[kv])
    hid = []
    for gl in range(2):
        a = acc[:, gl * 2 * CMP_HID: gl * 2 * CMP_HID + CMP_HID]
        b = acc[:, gl * 2 * CMP_HID + CMP_HID: (gl + 1) * 2 * CMP_HID]
        hid.append(jnp.maximum(a + pltpu.roll(b, n_chunks - 1, axis=0) + peb, 0.0))
    return _dot(jnp.concatenate(hid, axis=1).astype(BF16), w2_ref[kv])


def _compress_kernel(k_ref, v_ref, wp_ref, w1_ref, w2_ref, pe_ref, kc_ref, vct_ref, *, n_chunks):
    for kv, src_ref in ((0, k_ref), (1, v_ref)):
        load = lambda pos, src_ref=src_ref: src_ref[0, pl.ds(pos, n_chunks, stride=CMP_STRIDE), :]
        peb = _pe_bias(pe_ref, w1_ref, kv)
        out = _compress_pair(load, wp_ref, w2_ref, peb, kv, n_chunks)
        if kv == 0:
            kc_ref[0] = out.astype(BF16)
        else:
            vct_ref[0] = out.T.astype(BF16)


def _compress(kvc, wp, w1, w2, pe, *, seq0, B):
    T = kvc.shape[1]
    n_chunks = T // CMP_STRIDE
    n_pairs = KV_DIM // PAIR
    return pl.pallas_call(
        functools.partial(_compress_kernel, n_chunks=n_chunks),
        grid=(B, n_pairs),
        in_specs=[pl.BlockSpec((1, T, PAIR), lambda b, j: (seq0 + b, 0, j)),
                  pl.BlockSpec((1, T, PAIR), lambda b, j: (seq0 + b, 0, n_pairs + j)),
                  _const_spec(wp.shape), _const_spec(w1.shape), _const_spec(w2.shape), _const_spec(pe.shape)],
        out_specs=[pl.BlockSpec((1, n_chunks, PAIR), lambda b, j: (b, 0, j)),
                   pl.BlockSpec((1, PAIR, n_chunks), lambda b, j: (b, j, 0))],
        out_shape=[jax.ShapeDtypeStruct((B, n_chunks, KV_DIM), BF16),
                   jax.ShapeDtypeStruct((B, KV_DIM, n_chunks), BF16)],
        name="compress",
        compiler_params=pltpu.CompilerParams(dimension_semantics=("arbitrary", "arbitrary"),
                                             vmem_limit_bytes=VMEM_LIMIT),
    )(kvc, kvc, wp, w1, w2, pe)


def _build_q2(q, pr, tq):
    zero = jnp.zeros((tq, HEAD_DIM), F32)
    g0, g1 = [], []
    for r in range(GROUP):
        h0 = (2 * pr) * GROUP + r
        h1 = (2 * pr + 1) * GROUP + r
        g0.append(jnp.concatenate([q[:, h0 * HEAD_DIM:(h0 + 1) * HEAD_DIM].astype(F32), zero], axis=1))
        g1.append(jnp.concatenate([zero, q[:, h1 * HEAD_DIM:(h1 + 1) * HEAD_DIM].astype(F32)], axis=1))
    return jnp.concatenate(g0 + g1, axis=0).astype(BF16)


COL_BLOCK = 2 * LANES


class _Softmax:
    def __init__(self, m_ref, l_ref, acc_refs):
        self.m_ref, self.l_ref, self.acc_refs = m_ref, l_ref, acc_refs

    def reset(self):
        self.m_ref[...] = jnp.full(self.m_ref.shape, NEG, F32)
        self.l_ref[...] = jnp.zeros(self.l_ref.shape, F32)
        for acc_ref in self.acc_refs:
            acc_ref[...] = jnp.zeros(acc_ref.shape, F32)

    def update(self, score_fns, vt_fns, mask=None):
        m_all, l_all = self.m_ref[...], self.l_ref[...]
        m_out, l_out = [], []
        scores = [f() for f in score_fns]
        for k, s in enumerate(scores):
            cols = slice(k * COL_BLOCK, (k + 1) * COL_BLOCK)
            if mask is not None:
                s = jnp.where(mask, s, NEG)
            m_old = m_all[:, cols]
            m_new = jnp.maximum(m_old, jnp.max(s, axis=0, keepdims=True))
            alpha = jnp.exp2(m_old - m_new)
            p = jnp.exp2(s - m_new)
            l_out.append(alpha * l_all[:, cols] + jnp.sum(p, axis=0, keepdims=True))
            m_out.append(m_new)
            acc_ref = self.acc_refs[k]
            acc_ref[...] = alpha * acc_ref[...] + _dot(vt_fns[k](), p.astype(BF16))
        self.m_ref[...] = jnp.concatenate(m_out, axis=1)
        self.l_ref[...] = jnp.concatenate(l_out, axis=1)

    def result(self):
        acc = jnp.concatenate([acc_ref[...] for acc_ref in self.acc_refs], axis=1)
        return acc * (1.0 / jnp.maximum(self.l_ref[...], 1e-30))


def _attn_prompt_kernel(q_ref, hg_ref, kc_ref, vct_ref, ks_ref, vst_ref, kw_ref, vwt_ref, e_ref, smapt_ref,
                        o_ref, m_ref, l_ref, *acc_refs, tq, kc, n_cmp, n_sel):
    i = pl.program_id(1)
    q0 = i * tq
    n_pairs = KV_DIM // PAIR
    M = 2 * GROUP * tq
    blocks_per_pair = M // COL_BLOCK
    col = lax.broadcasted_iota(jnp.int32, (1, M), 1)
    qpos = q0 + (col & (tq - 1))
    qpos_t = qpos[:, 0:tq]
    qpos_blk = qpos[:, 0:COL_BLOCK]
    q = q_ref[...]
    hg_t = hg_ref[...].T
    n_chunks = kc_ref.shape[1]
    nsp = smapt_ref.shape[0]
    n_top = min(TOP_N, n_sel)
    n_row = lax.broadcasted_iota(jnp.int32, (n_chunks, 1), 0)
    j_row = lax.broadcasted_iota(jnp.int32, (nsp, 1), 0)
    sm = _Softmax(m_ref, l_ref, acc_refs)
    win_steps = WINDOW // tq
    pair_lanes = lambda pr: pl.ds(pr * PAIR, PAIR)
    group_rows = lambda pr, k: pl.ds(pr * PAIR + (k * 2 // blocks_per_pair) * HEAD_DIM, HEAD_DIM)
    blk_cols = lambda k: slice(k * COL_BLOCK, (k + 1) * COL_BLOCK)
    pair_blocks = [(pr, k) for pr in range(n_pairs) for k in range(blocks_per_pair)]

    q2s, rhs_sels, o_cmps = [], [], []
    for pr in range(n_pairs):
        q2 = _build_q2(q, pr, tq)
        q2s.append(q2)

        s = _dot_nt(kc_ref[0, :, pair_lanes(pr)], q2)
        cmask = (n_row * CMP_STRIDE + (CMP_BLOCK - 1) <= qpos) & (n_row < n_cmp)
        s = jnp.where(cmask, s, NEG)
        e = jnp.where(cmask, jnp.exp2(s - jnp.max(s, axis=0, keepdims=True)), 0.0)
        p_cmp = e * (1.0 / jnp.maximum(jnp.sum(e, axis=0, keepdims=True), 1e-30))
        o_cmps.append(jnp.concatenate([_dot(vct_ref[0, group_rows(pr, k), :], p_cmp[:, blk_cols(k)].astype(BF16))
                                       for k in range(blocks_per_pair)], axis=1))

        cur = qpos_t >> 6
        forced = (j_row == 0) | (j_row == cur) | (j_row == cur - 1)
        causal_blk = j_row * SEL_BLOCK <= qpos_t
        bias_rows = []
        for gl in range(2):
            psum = p_cmp[:, (gl * GROUP) * tq:(gl * GROUP + 1) * tq]
            for r in range(1, GROUP):
                psum = psum + p_cmp[:, (gl * GROUP + r) * tq:(gl * GROUP + r + 1) * tq]
            p_hi = psum.astype(BF16)
            p_lo = (psum - p_hi.astype(F32)).astype(BF16)
            imp = _dot(smapt_ref[...], p_hi) + _dot(smapt_ref[...], p_lo)
            imp = jnp.where(causal_blk, imp + jnp.where(forced, FORCE_BONUS, 0.0), NEG)
            imp = jnp.where(j_row < n_sel, imp, -jnp.inf)
            cnt = jnp.zeros((nsp, tq), F32)
            for jb in range(n_sel):
                ri = imp[jb:jb + 1, :]
                tie = jnp.where(j_row > jb, 1.0, 0.0)
                cnt = cnt + jnp.where(ri > imp, 1.0, jnp.where(ri == imp, tie, 0.0))
            bias_t = jnp.where(cnt < n_top, 0.0, -BIG)
            bias_t = jnp.concatenate([bias_t, jnp.zeros((LANES - nsp, tq), F32)], axis=0)
            bias_rows += [bias_t.T.astype(BF16)] * GROUP
        rhs_sels.append(jnp.concatenate([q2, jnp.concatenate(bias_rows, axis=0)], axis=1))

    def sel_update(c, mask=None):
        rows = pl.ds(pl.multiple_of(c * kc, kc), kc)
        one_hot = e_ref[rows, :]
        lhs = [jnp.concatenate([ks_ref[0, rows, pair_lanes(pr)], one_hot], axis=1) for pr in range(n_pairs)]
        sm.update([lambda pr=pr, k=k: _dot_nt(lhs[pr], rhs_sels[pr][blk_cols(k), :]) for pr, k in pair_blocks],
                  [lambda pr=pr, k=k: vst_ref[0, group_rows(pr, k), rows] for pr, k in pair_blocks], mask)

    sm.reset()
    c_diag = (q0 + tq - 1) // kc

    def sel_body(c, carry):
        sel_update(c)
        return carry
    lax.fori_loop(0, c_diag, sel_body, 0)
    sel_update(c_diag, c_diag * kc + lax.broadcasted_iota(jnp.int32, (kc, 1), 0) <= qpos_blk)
    o_sel = sm.result()

    def win_update(c, mask=None):
        rows = pl.ds(pl.multiple_of(c * tq, tq), tq)
        k2 = [kw_ref[0, rows, pair_lanes(pr)] for pr in range(n_pairs)]
        sm.update([lambda pr=pr, k=k: _dot_nt(k2[pr], q2s[pr][blk_cols(k), :]) for pr, k in pair_blocks],
                  [lambda pr=pr, k=k: vwt_ref[0, group_rows(pr, k), rows] for pr, k in pair_blocks], mask)

    sm.reset()
    k_row = lax.broadcasted_iota(jnp.int32, (tq, 1), 0)

    @pl.when(i >= win_steps)
    def _():
        win_update(i - win_steps, q0 - WINDOW + k_row > qpos_blk - WINDOW)

    def win_body(c, carry):
        win_update(c)
        return carry
    lax.fori_loop(jnp.maximum(i - win_steps + 1, 0), i, win_body, 0)
    win_update(i, q0 + k_row <= qpos_blk)
    o_win = sm.result()

    for pr in range(n_pairs):
        def gate_row(br):
            head0 = 2 * pr * GROUP
            return jnp.concatenate([hg_t[(head0 + j) * 3 + br:(head0 + j) * 3 + br + 1, :]
                                    for j in range(2 * GROUP)], axis=1)
        pair_cols = slice(pr * M, (pr + 1) * M)
        gated = gate_row(0) * o_cmps[pr] + gate_row(1) * o_sel[:, pair_cols] + gate_row(2) * o_win[:, pair_cols]
        for k in range(blocks_per_pair):
            c0 = k * COL_BLOCK
            two_heads = jnp.concatenate([gated[:, c0:c0 + tq], gated[:, c0 + tq:c0 + 2 * tq]], axis=0)
            o_blk = pr * blocks_per_pair + k
            o_ref[:, o_blk * LANES:(o_blk + 1) * LANES] = two_heads.T.astype(o_ref.dtype)


def _attn_prompt(q, hg, kcmp, vcmp_t, ks, vs_t, kw, vw_t, e_blk, smap_t, *, B, T, tq, kc):
    n_t = T // tq
    n_chunks = T // CMP_STRIDE
    n_cmp = n_chunks - CMP_BLOCK // CMP_STRIDE + 1
    n_sel = -(-T // SEL_BLOCK)
    assert T % kc == 0 and T % tq == 0 and kc % tq == 0 and WINDOW % tq == 0 and tq == LANES
    assert n_sel <= LANES and n_chunks <= LANES
    M = 2 * GROUP * tq
    D = q.shape[1]
    seq_rows = lambda width: pl.BlockSpec((1, T, width), lambda b, i: (b, 0, 0))
    seq_cols = pl.BlockSpec((1, KV_DIM, T), lambda b, i: (b, 0, 0))
    return pl.pallas_call(
        functools.partial(_attn_prompt_kernel, tq=tq, kc=kc, n_cmp=n_cmp, n_sel=n_sel),
        grid=(B, n_t),
        in_specs=[pl.BlockSpec((tq, D), lambda b, i: (b * n_t + i, 0)),
                  pl.BlockSpec((tq, LANES), lambda b, i: (b * n_t + i, 0)),
                  pl.BlockSpec((1, n_chunks, KV_DIM), lambda b, i: (b, 0, 0)),
                  pl.BlockSpec((1, KV_DIM, n_chunks), lambda b, i: (b, 0, 0)),
                  seq_rows(KV_DIM), seq_cols, seq_rows(KV_DIM), seq_cols,
                  _const_spec(e_blk.shape), _const_spec(smap_t.shape)],
        out_specs=pl.BlockSpec((tq, D), lambda b, i: (b * n_t + i, 0)),
        out_shape=jax.ShapeDtypeStruct((B * T, D), BF16),
        scratch_shapes=([pltpu.VMEM((1, 2 * M), F32), pltpu.VMEM((1, 2 * M), F32)]
                        + [pltpu.VMEM((HEAD_DIM, COL_BLOCK), F32)] * (2 * M // COL_BLOCK)),
        name="attn_prompt",
        compiler_params=pltpu.CompilerParams(dimension_semantics=("arbitrary", "arbitrary"),
                                             vmem_limit_bytes=VMEM_LIMIT),
    )(q, hg, kcmp, vcmp_t, ks, vs_t, kw, vw_t, e_blk, smap_t)


def _softmax_rows(parts):
    m = functools.reduce(jnp.maximum, [jnp.max(s, axis=1, keepdims=True) for s in parts])
    ps = [jnp.exp2(s - m) for s in parts]
    inv = 1.0 / jnp.maximum(functools.reduce(jnp.add, [jnp.sum(p, axis=1, keepdims=True) for p in ps]), 1e-30)
    return [p * inv for p in ps]


def _attn_sample_kernel(pt_ref, q_ref, hg_ref, kvs_new_ref, kvw_new_ref, win_ref, cache_c_ref, cache_s_ref,
                        wp_ref, w1_ref, w2_ref, pe_ref, smap_ref, e_ref,
                        o_ref,
                        buf_c, buf_s, peb_ref, sem_c, sem_s,
                        *, layer, n_pages, page, past_len, tq, n_cmp, n_sel, wb):
    b = pl.program_id(0)
    nb = pl.num_programs(0)
    slot = b % 2

    def page_copies(bb, sl):
        out = []
        for p in range(n_pages):
            pg = pt_ref[bb, p]
            for j in range(KV_ROW // PAIR):
                out.append(pltpu.make_async_copy(cache_c_ref.at[layer, pg, :, pl.ds(j * PAIR, PAIR)],
                                                 buf_c.at[sl, j, pl.ds(p * page, page)], sem_c.at[sl]))
            out.append(pltpu.make_async_copy(cache_s_ref.at[layer, pg], buf_s.at[sl, pl.ds(p * page, page)],
                                             sem_s.at[sl]))
        return out

    @pl.when(b == 0)
    def _():
        for cp in page_copies(0, 0):
            cp.start()
        for kv in range(2):
            peb_ref[kv] = jnp.broadcast_to(_pe_bias(pe_ref, w1_ref, kv), (8, CMP_HID))

    @pl.when(b + 1 < nb)
    def _():
        for cp in page_copies(b + 1, 1 - slot):
            cp.start()

    for cp in page_copies(b, slot):
        cp.wait()

    M = N_HEADS * tq
    n_new = LANES
    pad_new = lambda ref: jnp.concatenate([ref[0], jnp.zeros((n_new - ref.shape[1], KV_ROW), BF16)], axis=0)
    kvs_new = pad_new(kvs_new_ref)
    kvw_new = pad_new(kvw_new_ref)
    row = lax.broadcasted_iota(jnp.int32, (M, 1), 0)
    qpos = past_len + (row & (tq - 1))
    lane = lax.broadcasted_iota(jnp.int32, (1, LANES), 1)
    q = q_ref[0].astype(F32)
    hg = hg_ref[0]
    n_chunks = past_len // CMP_STRIDE

    blocks = []
    for h in range(N_HEADS):
        g = h // GROUP
        pieces = [jnp.zeros((tq, HEAD_DIM * g), F32)] if g else []
        pieces.append(q[:, h * HEAD_DIM:(h + 1) * HEAD_DIM])
        if g < N_KV - 1:
            pieces.append(jnp.zeros((tq, HEAD_DIM * (N_KV - 1 - g)), F32))
        blocks.append(jnp.concatenate(pieces, axis=1) if len(pieces) > 1 else pieces[0])
    q4 = jnp.concatenate(blocks, axis=0).astype(BF16)

    cmp_kv = []
    for kv in range(2):
        halves = []
        for pr in range(KV_DIM // PAIR):
            load = lambda pos, j=kv * (KV_DIM // PAIR) + pr: buf_c[slot, j, pl.ds(pos, n_chunks, stride=CMP_STRIDE), :]
            halves.append(_compress_pair(load, wp_ref, w2_ref, peb_ref[kv][0:1, :], kv, n_chunks))
        cmp_kv.append(jnp.concatenate(halves, axis=1).astype(BF16))

    s = _dot_nt(q4, cmp_kv[0])
    n_idx = lax.broadcasted_iota(jnp.int32, (1, n_chunks), 1)
    cmask = (n_idx * CMP_STRIDE + (CMP_BLOCK - 1) <= qpos) & (n_idx < n_cmp)
    s = jnp.where(cmask, s, NEG)
    e = jnp.where(cmask, jnp.exp2(s - jnp.max(s, axis=1, keepdims=True)), 0.0)
    p_cmp = e * (1.0 / jnp.maximum(jnp.sum(e, axis=1, keepdims=True), 1e-30))
    o_cmp = _dot(p_cmp.astype(BF16), cmp_kv[1])

    psum = []
    for g in range(N_KV):
        acc = p_cmp[(g * GROUP) * tq:(g * GROUP + 1) * tq, :]
        for r in range(1, GROUP):
            acc = acc + p_cmp[(g * GROUP + r) * tq:(g * GROUP + r + 1) * tq, :]
        psum.append(acc)
    psum = jnp.concatenate(psum, axis=0)
    p_hi = psum.astype(BF16)
    p_lo = (psum - p_hi.astype(F32)).astype(BF16)
    imp = _dot(p_hi, smap_ref[...]) + _dot(p_lo, smap_ref[...])
    qpos_g = qpos[0:N_KV * tq, :]
    cur = qpos_g >> 6
    forced = (lane == 0) | (lane == cur) | (lane == cur - 1)
    imp = jnp.where(lane * SEL_BLOCK <= qpos_g, imp + jnp.where(forced, FORCE_BONUS, 0.0), NEG)
    imp = jnp.where(lane < n_sel, imp, -jnp.inf)
    cnt = jnp.zeros(imp.shape, F32)
    for jb in range(n_sel):
        ci = imp[:, jb:jb + 1]
        tie = jnp.where(lane > jb, 1.0, 0.0)
        cnt = cnt + jnp.where(ci > imp, 1.0, jnp.where(ci == imp, tie, 0.0))
    neg = jnp.where(cnt < min(TOP_N, n_sel), 0.0, -BIG)
    neg = jnp.concatenate([neg[g * tq:(g + 1) * tq, :] for g in range(N_KV) for _ in range(GROUP)], axis=0)
    bias = _dot(neg.astype(BF16), e_ref[...])

    new_pos = past_len + lax.broadcasted_iota(jnp.int32, (1, n_new), 1)
    s_old = _dot_nt(q4, buf_s[slot, :, 0:KV_DIM].astype(BF16)) + bias[:, 0:past_len]
    s_new = _dot_nt(q4, kvs_new[:, 0:KV_DIM]) + bias[:, past_len:past_len + n_new]
    p_old, p_new = _softmax_rows([s_old, jnp.where(new_pos <= qpos, s_new, NEG)])
    o_sel = (_dot(p_old.astype(BF16), buf_s[slot, :, KV_DIM:KV_ROW].astype(BF16))
             + _dot(p_new.astype(BF16), kvs_new[:, KV_DIM:KV_ROW]))

    st_pos = past_len - wb + lax.broadcasted_iota(jnp.int32, (1, wb), 1)
    s_st = jnp.where(st_pos > qpos - WINDOW, _dot_nt(q4, win_ref[0, :, 0:KV_DIM].astype(BF16)), NEG)
    s_nw = jnp.where(new_pos <= qpos, _dot_nt(q4, kvw_new[:, 0:KV_DIM]), NEG)
    p_st, p_nw = _softmax_rows([s_st, s_nw])
    o_win = (_dot(p_st.astype(BF16), win_ref[0, :, KV_DIM:KV_ROW].astype(BF16))
             + _dot(p_nw.astype(BF16), kvw_new[:, KV_DIM:KV_ROW]))

    gate = lambda br: jnp.concatenate([hg[:, h * 3 + br:h * 3 + br + 1] for h in range(N_HEADS)], axis=0)
    gated = gate(0) * o_cmp + gate(1) * o_sel + gate(2) * o_win

    for blk in range(N_HEADS // 2):
        g = (2 * blk) // GROUP
        lanes = slice((g // 2) * PAIR, (g // 2 + 1) * PAIR)
        even = gated[(2 * blk) * tq:(2 * blk + 1) * tq, lanes]
        odd = gated[(2 * blk + 1) * tq:(2 * blk + 2) * tq, lanes]
        if g % 2 == 0:
            odd = pltpu.roll(odd, HEAD_DIM, axis=1)
        else:
            even = pltpu.roll(even, HEAD_DIM, axis=1)
        o_ref[0, :, blk * LANES:(blk + 1) * LANES] = jnp.where(lane < HEAD_DIM, even, odd).astype(o_ref.dtype)


def _attn_sample(page_table, q, hg, kvs_new, kvw_new, win_state, cache_c, cache_s, wp, w1, w2, pe, smap, e_blk,
                 *, layer, t_new):
    nb, n_pages = page_table.shape
    page = cache_s.shape[2]
    past_len = n_pages * page
    wb = win_state.shape[2]
    tq = q.shape[1]
    n_new = kvs_new.shape[1]
    D = q.shape[2]
    L = past_len + t_new
    n_chunks = L // CMP_STRIDE
    assert past_len % CMP_STRIDE == 0 and n_chunks * CMP_STRIDE == past_len and n_chunks <= LANES
    assert t_new <= tq == 8 and t_new <= n_new <= LANES and wb <= WINDOW and e_blk.shape == (LANES, past_len + LANES)
    n_cmp = n_chunks - CMP_BLOCK // CMP_STRIDE + 1
    n_sel = -(-L // SEL_BLOCK)
    assert n_sel <= LANES
    grid_spec = pltpu.PrefetchScalarGridSpec(
        num_scalar_prefetch=1, grid=(nb,),
        in_specs=[pl.BlockSpec((1, tq, D), lambda b, pt: (b, 0, 0)),
                  pl.BlockSpec((1, tq, LANES), lambda b, pt: (b, 0, 0)),
                  pl.BlockSpec((1, n_new, KV_ROW), lambda b, pt: (b, 0, 0)),
                  pl.BlockSpec((1, n_new, KV_ROW), lambda b, pt: (b, 0, 0)),
                  pl.BlockSpec((None, 1, wb, KV_ROW), lambda b, pt: (layer, b, 0, 0)),
                  pl.BlockSpec(memory_space=pl.ANY),
                  pl.BlockSpec(memory_space=pl.ANY),
                  _const_spec(wp.shape), _const_spec(w1.shape), _const_spec(w2.shape), _const_spec(pe.shape),
                  _const_spec(smap.shape), _const_spec(e_blk.shape)],
        out_specs=pl.BlockSpec((1, tq, D), lambda b, pt: (b, 0, 0)),
        scratch_shapes=[pltpu.VMEM((2, KV_ROW // PAIR, past_len, PAIR), F32),
                        pltpu.VMEM((2, past_len, KV_ROW), F32),
                        pltpu.VMEM((2, 8, CMP_HID), F32),
                        pltpu.SemaphoreType.DMA((2,)), pltpu.SemaphoreType.DMA((2,))])
    return pl.pallas_call(
        functools.partial(_attn_sample_kernel, layer=layer, n_pages=n_pages, page=page, past_len=past_len,
                          tq=tq, n_cmp=n_cmp, n_sel=n_sel, wb=wb),
        grid_spec=grid_spec,
        out_shape=jax.ShapeDtypeStruct((nb, tq, D), BF16),
        name="attn_sample",
        compiler_params=pltpu.CompilerParams(dimension_semantics=("arbitrary",), vmem_limit_bytes=VMEM_LIMIT),
    )(page_table, q, hg, kvs_new, kvw_new, win_state, cache_c, cache_s, wp, w1, w2, pe, smap, e_blk)


def _post_kernel(o_ref, yc_ref, ga_ref, gc_ref, x_ref, g_ref, wa_ref, wc_ref, wo_ref, w1_ref, w2_ref, out_ref,
                 *, ff_chunk):
    g = g_ref[...]
    a = _dot(o_ref[...], wa_ref[...])
    c = _dot(yc_ref[...].astype(BF16), wc_ref[...])
    mix = _dot((ga_ref[...] * a + gc_ref[...] * c).astype(BF16), wo_ref[...])
    x1 = x_ref[...] + _rms(mix, g[1:2])
    h2 = _rms(x1, g[2:3]).astype(BF16)
    f = jnp.zeros(x1.shape, F32)
    for j in range(w1_ref.shape[1] // ff_chunk):
        t = jnp.maximum(_dot(h2, w1_ref[:, j * ff_chunk:(j + 1) * ff_chunk]), 0.0)
        f = f + _dot((t * t).astype(BF16), w2_ref[j * ff_chunk:(j + 1) * ff_chunk, :])
    out_ref[...] = x1 + _rms(f, g[3:4])


def _post(o, yc, ga, gc, x, g, wa, wc, wo, w1, w2, *, tm):
    M, D = x.shape
    row_spec = pl.BlockSpec((tm, D), lambda i: (i, 0))
    return pl.pallas_call(
        functools.partial(_post_kernel, ff_chunk=1024),
        grid=(M // tm,),
        in_specs=[row_spec] * 5 + [_const_spec(a.shape) for a in (g, wa, wc, wo, w1, w2)],
        out_specs=row_spec,
        out_shape=jax.ShapeDtypeStruct((M, D), F32),
        name="post",
        compiler_params=pltpu.CompilerParams(dimension_semantics=("arbitrary",), vmem_limit_bytes=VMEM_LIMIT),
    )(o, yc, ga, gc, x, g, wa, wc, wo, w1, w2)


def _sel_map(n_chunks, n_cmp, n_sel):
    cs = np.arange(n_chunks)[:, None] * CMP_STRIDE
    ss = np.arange(LANES)[None, :] * SEL_BLOCK
    ov = np.clip(np.minimum(cs + CMP_BLOCK, ss + SEL_BLOCK) - np.maximum(cs, ss), 0, None) / CMP_STRIDE
    ov = ov * (np.arange(n_chunks)[:, None] < n_cmp) * (np.arange(LANES)[None, :] < n_sel)
    return jnp.asarray(ov, dtype=BF16)


def _prep_weights(w_in, cmp_pe, cmp_w1, cmp_w2):
    D = w_in.shape[1]
    assert D == _D
    kvd = 2 * KV_DIM
    o_q, o_c, o_s, o_w, o_h = 0, D, D + kvd, D + 2 * kvd, D + 3 * kvd
    o_b = o_h + 3 * N_HEADS
    o_cg, o_x, o_m = o_b + D, o_b + 2 * D, o_b + 3 * D
    sl = lambda a, n: w_in[:, :, a:a + n]
    hg_pad = jnp.pad(sl(o_h, 3 * N_HEADS), ((0, 0), (0, 0), (0, LANES - 3 * N_HEADS)))
    w_cat = jnp.concatenate([sl(o_q, D), sl(o_b, D), sl(o_cg, D), sl(o_x, D), sl(o_m, D), sl(o_m + D, D),
                             sl(o_c, kvd), sl(o_s, kvd), sl(o_w, kvd), hg_pad], axis=2).astype(BF16)
    depth = w_in.shape[0]
    half = CMP_STRIDE * HEAD_DIM
    w1r = cmp_w1.reshape(depth, 2, 2, CMP_STRIDE, HEAD_DIM, CMP_HID)
    ab = jnp.concatenate([w1r[:, :, 0], w1r[:, :, 1]], axis=-1)
    z = jnp.zeros_like(ab)
    wp = jnp.concatenate([jnp.concatenate([ab, z], axis=-1), jnp.concatenate([z, ab], axis=-1)], axis=-2)
    wp = wp.reshape(depth, 2, CMP_STRIDE * PAIR, 4 * CMP_HID)
    z2 = jnp.zeros_like(cmp_w2)
    w2bd = jnp.concatenate([jnp.concatenate([cmp_w2, z2], axis=-1), jnp.concatenate([z2, cmp_w2], axis=-1)], axis=-2)
    pe8 = jnp.broadcast_to(cmp_pe.reshape(depth, 2, 1, 2 * half), (depth, 2, 8, 2 * half))
    return w_cat, wp.astype(BF16), cmp_w1.astype(BF16), w2bd.astype(BF16), pe8.astype(BF16)


def kernel(x_prompt, x_sample, cache_cmp_kv, cache_sel_kv, state_win_kv, state_conv, page_table, norm_g, w_in,
           cmp_pe, cmp_w1, cmp_w2, conv_w, w_up_attn, w_up_conv, w_o, w_ff1, w_ff2):
    B, T, D = x_prompt.shape
    NB, TS, _ = x_sample.shape
    depth = w_in.shape[0]
    n_pool, page = cache_cmp_kv.shape[1:3]
    wb = state_win_kv.shape[2]
    past_len = page_table.shape[1] * page
    tq_s = 8

    w_cat, wp, w1b, w2bd, pe8 = _prep_weights(w_in, cmp_pe, cmp_w1, cmp_w2)
    wa, wc, wo, wf1, wf2 = (w.astype(BF16) for w in (w_up_attn, w_up_conv, w_o, w_ff1, w_ff2))
    cache_c = cache_cmp_kv.reshape(depth, n_pool, page, KV_ROW)
    cache_s = cache_sel_kv.reshape(depth, n_pool, page, KV_ROW)
    win_state = state_win_kv.reshape(depth, NB, wb, KV_ROW)
    zs = lambda n: jnp.zeros((depth, NB, n, D), F32)
    p1 = jnp.concatenate([state_conv[:, :, 1:2], zs(TS - 1)], axis=2).reshape(depth, NB * TS, D)
    p2 = jnp.concatenate([state_conv, zs(TS - 2)], axis=2).reshape(depth, NB * TS, D)

    n_chunks_p = T // CMP_STRIDE
    n_cmp_p = n_chunks_p - CMP_BLOCK // CMP_STRIDE + 1
    n_sel_p = -(-T // SEL_BLOCK)
    smap_pt = _sel_map(n_chunks_p, n_cmp_p, n_sel_p).T[:-(-n_sel_p // 8) * 8]
    e_blk = jnp.asarray(np.arange(T)[:, None] // SEL_BLOCK == np.arange(LANES)[None, :], dtype=BF16)
    q_scale = HEAD_DIM ** -0.5 * LOG2E
    L = past_len + TS
    n_chunks_s = L // CMP_STRIDE
    smap_s = _sel_map(n_chunks_s, n_chunks_s - CMP_BLOCK // CMP_STRIDE + 1, -(-L // SEL_BLOCK))
    e_blk_s = jnp.asarray(np.arange(LANES)[:, None] == np.arange(past_len + LANES)[None, :] // SEL_BLOCK, dtype=BF16)

    tm_p = min(256, T)
    tq_p = min(128, T)
    kc = 256
    pad_t = lambda a, n: jnp.pad(a.reshape(NB, TS, a.shape[-1]), ((0, 0), (0, n - TS), (0, 0)))

    xp = x_prompt.reshape(B * T, D)
    xs = x_sample.reshape(NB * TS, D)
    outs = [[] for _ in range(8)]
    stacks = [jnp.zeros((depth * B * T, KV_ROW), F32) for _ in range(3)]
    for l in range(depth):
        g = norm_g[l]
        (q, yc, ga, gc, *stacks, ks_b, vs_t, kw_b, vw_t, hg, ust) = _proj(
            xp, g[0:1], w_cat[l], conv_w[l], tm=tm_p, seq_len=T, q_scale=q_scale, stacks=stacks, layer=l)
        kcmp, vcmp_t = _compress(stacks[0].reshape(depth * B, T, KV_ROW), wp[l], w1b[l], w2bd[l], pe8[l],
                                 seq0=l * B, B=B)
        o = _attn_prompt(q, hg, kcmp, vcmp_t, ks_b.reshape(B, T, KV_DIM), vs_t, kw_b.reshape(B, T, KV_DIM), vw_t,
                         e_blk, smap_pt, B=B, T=T, tq=tq_p, kc=min(kc, T))
        xp = _post(o, yc, ga, gc, xp, g, wa[l], wc[l], wo[l], wf1[l], wf2[l], tm=tm_p)
        kv5 = lambda a, n: a.reshape(n, -1, 2, N_KV, HEAD_DIM)
        outs[6].append(ust[:, 8 - (CONV_W - 1):])
        (q, yc, ga, gc, kvc, kvs, kvw, kvs_b, kvw_b, hg, u) = _proj(
            xs, g[0:1], w_cat[l], conv_w[l], tm=NB * TS, seq_len=TS, q_scale=q_scale, prefix=(p1[l], p2[l]))
        o = _attn_sample(page_table, pad_t(q, tq_s), pad_t(hg, tq_s), pad_t(kvs_b, 16), pad_t(kvw_b, 16), win_state,
                         cache_c, cache_s, wp[l], w1b[l], w2bd[l], pe8[l], smap_s, e_blk_s, layer=l, t_new=TS)
        o = o[:, :TS].reshape(NB * TS, D)
        xs = _post(o, yc, ga, gc, xs, g, wa[l], wc[l], wo[l], wf1[l], wf2[l], tm=min(256, NB * TS))
        outs[1].append(kv5(kvc, NB))
        outs[3].append(kv5(kvs, NB))
        outs[5].append(kv5(kvw, NB))
        outs[7].append(u.reshape(NB, TS, D)[:, TS - (CONV_W - 1):])
    kv6 = lambda a: a.reshape(depth, B, T, 2, N_KV, HEAD_DIM)
    outs[0], outs[2], outs[4] = kv6(stacks[0]), kv6(stacks[1]), kv6(stacks[2])[:, :, -min(WINDOW, T):]
    st = [o if not isinstance(o, list) else jnp.stack(o) for o in outs]
    st[5] = jnp.concatenate([state_win_kv, st[5]], axis=2)[:, :, -wb:]
    return (xp.reshape(B, T, D), xs.reshape(NB, TS, D), st[0], st[1], st[2], st[3], st[4], st[5], st[6], st[7])
```
